```python
import math
import jax, jax.numpy as jnp
from jax import lax
import numpy as np

D_MODEL = 4096
BATCH = 1
SEQ = 16384
DEPTH = 4

GRID_W = 64
CTX_LEN = 256
N_MIXERS = 2
NORM_EPS = 1e-6
S5_GROUP = 16
S5_GROUPS = D_MODEL // S5_GROUP
S5_STATE = 64
S5_DIRS = 2
SCAN_CHUNK = 128
DT_MIN = 1e-3
DT_MAX = 1e-1
NA_HEADS = 32
NA_HEAD_DIM = D_MODEL // NA_HEADS
NA_KH = 8
NA_KW = 16
FFN_DIM = 4096
N_EXPERTS = 8
TOP_K = 2
EXPERT_DIM = 1024
N_EVEN_LAYERS = (DEPTH + 1) // 2
N_ODD_LAYERS = DEPTH // 2

kernel_name = "hybrid_s5_natten_moe_dit"


def _rms_norm(x, g):
    xf = x.astype(jnp.float32)
    y = xf * lax.rsqrt(jnp.mean(xf * xf, axis=-1, keepdims=True) + NORM_EPS)
    return (y * g.astype(jnp.float32)).astype(x.dtype)


def _adaln(cond, w, b):
    m = (jax.nn.silu(cond) @ w + b)[:, None, :]
    return jnp.split(m, 6, axis=-1)


def _modulate(h, shift, scale):
    return h * (1 + scale) + shift


def _swiglu(h, w_gu, w_down):
    gate, up = jnp.split(h @ w_gu, 2, axis=-1)
    return (jax.nn.silu(gate) * up) @ w_down


def _moe(h, w_router, w_gu, w_down):
    logits = (h @ w_router).astype(jnp.float32)
    top_val, top_idx = lax.top_k(logits, TOP_K)
    top_w = jax.nn.softmax(top_val, axis=-1)
    gates = jnp.einsum('btk,btke->bte', top_w,
                       jax.nn.one_hot(top_idx, N_EXPERTS, dtype=jnp.float32)).astype(h.dtype)
    out = jnp.zeros_like(h)
    for e in range(N_EXPERTS):
        out = out + gates[..., e:e + 1] * _swiglu(h, w_gu[e], w_down[e])
    return out


def _ssm_combine(left, right):
    a_l, b_l = left
    a_r, b_r = right
    return a_l * a_r, a_r * b_l + b_r


def _s5_scan(u, h0, lam, dt, b_mat, c_mat):
    bsz, length = u.shape[:2]
    n_chunks = length // SCAN_CHUNK
    lam_dt = lam * dt[:, None]
    a_bar = jnp.exp(lam_dt)
    b_bar = ((a_bar - 1.0) / lam)[..., None] * b_mat
    steps = jnp.arange(1, SCAN_CHUNK + 1, dtype=jnp.float32)[:, None, None]
    a_pow = jnp.exp(lam_dt[None] * steps)
    u_chunks = jnp.moveaxis(u.reshape(bsz, n_chunks, SCAN_CHUNK, S5_GROUPS, S5_GROUP), 1, 0)

    def chunk_step(h, u_c):
        bu = jnp.einsum('btgc,gnc->btgn', u_c.astype(jnp.complex64), b_bar)
        a = jnp.broadcast_to(a_bar, bu.shape)
        _, h_loc = lax.associative_scan(_ssm_combine, (a, bu), axis=1)
        h_all = h_loc + a_pow[None] * h[:, None]
        y = jnp.einsum('btgn,gcn->btgc', h_all, c_mat).real
        return h_all[:, -1], y

    h_last, y = lax.scan(chunk_step, h0, u_chunks)
    return jnp.moveaxis(y, 0, 1).reshape(bsz, length, S5_GROUPS, S5_GROUP), h_last


def _s5_glu(y, w_glu):
    g = jax.nn.gelu(y)
    a, b = jnp.split(g @ w_glu, 2, axis=-1)
    return a * jax.nn.sigmoid(b)


def _s5_mixer(hx, hc, a_re, a_im, log_dt, b_re, b_im, c_re, c_im, d_skip, w_glu, with_ctx_out):
    bsz, seq, _ = hx.shape
    f32 = jnp.float32
    ux = hx.astype(f32).reshape(bsz, seq, S5_GROUPS, S5_GROUP)
    uc = hc.astype(f32).reshape(bsz, CTX_LEN, S5_GROUPS, S5_GROUP)
    d_g = d_skip.astype(f32).reshape(S5_GROUPS, S5_GROUP)
    yx = d_g * ux
    yc = d_g * uc
    for direction in range(S5_DIRS):
        lam = lax.complex(a_re[direction].astype(f32), a_im[direction].astype(f32))
        dt = jnp.exp(log_dt[direction].astype(f32))
        b_mat = lax.complex(b_re[direction].astype(f32), b_im[direction].astype(f32))
        c_mat = lax.complex(c_re[direction].astype(f32), c_im[direction].astype(f32))
        rev = direction == 1
        uc_d = uc[:, ::-1] if rev else uc
        ux_d = ux[:, ::-1] if rev else ux
        h0 = jnp.zeros((bsz, S5_GROUPS, S5_STATE), jnp.complex64)
        yc_d, h_ctx = _s5_scan(uc_d, h0, lam, dt, b_mat, c_mat)
        yx_d, _ = _s5_scan(ux_d, h_ctx, lam, dt, b_mat, c_mat)
        if rev:
            yc_d = yc_d[:, ::-1]
            yx_d = yx_d[:, ::-1]
        yx = yx + yx_d
        yc = yc + yc_d
    out_x = _s5_glu(yx.reshape(bsz, seq, D_MODEL).astype(hx.dtype), w_glu)
    out_c = _s5_glu(yc.reshape(bsz, CTX_LEN, D_MODEL).astype(hc.dtype), w_glu) if with_ctx_out else None
    return out_x, out_c


def _na_mixer(hx, hc, w_qkv, w_o, rpb, with_ctx_out):
    bsz, seq, _ = hx.shape
    rows = seq // GRID_W
    kh = min(NA_KH, rows)
    scale = NA_HEAD_DIM ** -0.5
    qkv = (hx @ w_qkv).reshape(bsz, seq, 3, NA_HEADS, NA_HEAD_DIM)
    qg = qkv[:, :, 0].reshape(bsz, rows, GRID_W, NA_HEADS, NA_HEAD_DIM)
    kg = qkv[:, :, 1].reshape(bsz, rows, GRID_W, NA_HEADS, NA_HEAD_DIM)
    vg = qkv[:, :, 2].reshape(bsz, rows, GRID_W, NA_HEADS, NA_HEAD_DIM)
    qkv_c = (hc @ w_qkv).reshape(bsz, CTX_LEN, 3, NA_HEADS, NA_HEAD_DIM)
    qc, kc, vc = qkv_c[:, :, 0], qkv_c[:, :, 1], qkv_c[:, :, 2]

    cols = jnp.arange(GRID_W)
    col_start = jnp.clip(cols - NA_KW // 2, 0, GRID_W - NA_KW)
    col_idx = col_start[:, None] + jnp.arange(NA_KW)[None, :]
    col_rel = col_idx - cols[:, None] + (NA_KW - 1)
    col_bias = rpb[:, :, col_rel]
    n_win = kh * NA_KW

    def row_block(r):
        r0 = jnp.clip(r - kh // 2, 0, rows - kh)
        k_band = lax.dynamic_slice_in_dim(kg, r0, kh, axis=1)
        v_band = lax.dynamic_slice_in_dim(vg, r0, kh, axis=1)
        k_win = k_band[:, :, col_idx]
        v_win = v_band[:, :, col_idx]
        q_r = lax.dynamic_index_in_dim(qg, r, axis=1, keepdims=False)
        row_rel = r0 + jnp.arange(kh) - r + (NA_KH - 1)
        bias = jnp.take(col_bias, row_rel, axis=1).transpose(0, 2, 1, 3)
        s_win = jnp.einsum('bqhd,bjqkhd->bhqjk', q_r, k_win).astype(jnp.float32) * scale
        s_win = s_win + bias[None].astype(jnp.float32)
        s_ctx = jnp.einsum('bqhd,bchd->bhqc', q_r, kc).astype(jnp.float32) * scale
        s = jnp.concatenate([s_win.reshape(bsz, NA_HEADS, GRID_W, n_win), s_ctx], axis=-1)
        p = jax.nn.softmax(s, axis=-1).astype(vg.dtype)
        p_win = p[..., :n_win].reshape(bsz, NA_HEADS, GRID_W, kh, NA_KW)
        return (jnp.einsum('bhqjk,bjqkhd->bqhd', p_win, v_win)
                + jnp.einsum('bhqc,bchd->bqhd', p[..., n_win:], vc))

    o = lax.map(row_block, jnp.arange(rows))
    out_x = jnp.moveaxis(o, 0, 1).reshape(bsz, seq, D_MODEL) @ w_o
    out_c = None
    if with_ctx_out:
        s = jnp.einsum('bqhd,bkhd->bhqk', qc, kc).astype(jnp.float32) * scale
        p = jax.nn.softmax(s, axis=-1).astype(vc.dtype)
        out_c = jnp.einsum('bhqk,bkhd->bqhd', p, vc).reshape(bsz, CTX_LEN, D_MODEL) @ w_o
    return out_x, out_c


def setup_inputs(seed: int = 0) -> dict:
    key = jax.random.key(seed)
    ks = jax.random.split(key, 32)
    f32 = jnp.float32

    def nrm(k, shape, scale):
        return jax.random.normal(k, shape, f32) * scale

    ne, no = N_EVEN_LAYERS, N_ODD_LAYERS
    s5_shape = (ne, S5_DIRS, S5_GROUPS, S5_STATE)
    state_idx = jnp.arange(S5_STATE, dtype=f32)
    return {
        "x": nrm(ks[0], (BATCH, SEQ, D_MODEL), 1.0),
        "c": nrm(ks[1], (BATCH, D_MODEL), 1.0),
        "ctx": nrm(ks[2], (BATCH, CTX_LEN, D_MODEL), 1.0),
        "c_ctx": nrm(ks[3], (D_MODEL,), 1.0),
        "ada_w": nrm(ks[4], (DEPTH, D_MODEL, 6 * D_MODEL), 0.5 * D_MODEL ** -0.5),
        "ada_b": nrm(ks[5], (DEPTH, 6 * D_MODEL), 0.02),
        "norm_g": 1.0 + nrm(ks[6], (DEPTH, 4, D_MODEL), 0.05),
        "s5_a_re": -0.5 + nrm(ks[7], s5_shape, 0.01),
        "s5_a_im": math.pi * state_idx + nrm(ks[8], s5_shape, 0.01),
        "s5_log_dt": jax.random.uniform(ks[9], (ne, S5_DIRS, S5_GROUPS), f32,
                                        math.log(DT_MIN), math.log(DT_MAX)),
        "s5_b_re": nrm(ks[10], (ne, S5_DIRS, S5_GROUPS, S5_STATE, S5_GROUP), (2 * S5_GROUP) ** -0.5),
        "s5_b_im": nrm(ks[11], (ne, S5_DIRS, S5_GROUPS, S5_STATE, S5_GROUP), (2 * S5_GROUP) ** -0.5),
        "s5_c_re": nrm(ks[12], (ne, S5_DIRS, S5_GROUPS, S5_GROUP, S5_STATE), (2 * S5_STATE) ** -0.5),
        "s5_c_im": nrm(ks[13], (ne, S5_DIRS, S5_GROUPS, S5_GROUP, S5_STATE), (2 * S5_STATE) ** -0.5),
        "s5_d": nrm(ks[14], (ne, D_MODEL), 1.0),
        "s5_w_glu": nrm(ks[15], (ne, D_MODEL, 2 * D_MODEL), D_MODEL ** -0.5),
        "na_w_qkv": nrm(ks[16], (no, D_MODEL, 3 * D_MODEL), D_MODEL ** -0.5),
        "na_w_o": nrm(ks[17], (no, D_MODEL, D_MODEL), D_MODEL ** -0.5),
        "na_rpb": nrm(ks[18], (no, NA_HEADS, 2 * NA_KH - 1, 2 * NA_KW - 1), 0.02),
        "ffn_w_gu": nrm(ks[19], (ne, D_MODEL, 2 * FFN_DIM), D_MODEL ** -0.5),
        "ffn_w_down": nrm(ks[20], (ne, FFN_DIM, D_MODEL), FFN_DIM ** -0.5),
        "moe_w_router": nrm(ks[21], (no, D_MODEL, N_EXPERTS), D_MODEL ** -0.5),
        "moe_w_gu": nrm(ks[22], (no, N_EXPERTS, D_MODEL, 2 * EXPERT_DIM), D_MODEL ** -0.5),
        "moe_w_down": nrm(ks[23], (no, N_EXPERTS, EXPERT_DIM, D_MODEL), EXPERT_DIM ** -0.5),
    }


def reference(x, c, ctx, c_ctx, ada_w, ada_b, norm_g,
              s5_a_re, s5_a_im, s5_log_dt, s5_b_re, s5_b_im, s5_c_re, s5_c_im, s5_d, s5_w_glu,
              na_w_qkv, na_w_o, na_rpb,
              ffn_w_gu, ffn_w_down,
              moe_w_router, moe_w_gu, moe_w_down):
    for i in range(DEPTH):
        j = i // N_MIXERS
        with_ctx_out = i < DEPTH - 1
        sh_m, sc_m, g_m, sh_f, sc_f, g_f = _adaln(c, ada_w[i], ada_b[i])
        csh_m, csc_m, cg_m, csh_f, csc_f, cg_f = _adaln(c_ctx[None], ada_w[i], ada_b[i])

        hx = _modulate(_rms_norm(x, norm_g[i, 0]), sh_m, sc_m)
        hc = _modulate(_rms_norm(ctx, norm_g[i, 0]), csh_m, csc_m)
        if i % N_MIXERS == 0:
            yx, yc = _s5_mixer(hx, hc, s5_a_re[j], s5_a_im[j], s5_log_dt[j], s5_b_re[j], s5_b_im[j],
                               s5_c_re[j], s5_c_im[j], s5_d[j], s5_w_glu[j], with_ctx_out)
        else:
            yx, yc = _na_mixer(hx, hc, na_w_qkv[j], na_w_o[j], na_rpb[j], with_ctx_out)
        x = x + g_m * _rms_norm(yx, norm_g[i, 1])
        if with_ctx_out:
            ctx = ctx + cg_m * _rms_norm(yc, norm_g[i, 1])

        hx = _modulate(_rms_norm(x, norm_g[i, 2]), sh_f, sc_f)
        if with_ctx_out:
            hc = _modulate(_rms_norm(ctx, norm_g[i, 2]), csh_f, csc_f)
            h = jnp.concatenate([hc, hx], axis=1)
        else:
            h = hx
        if i % 2 == 0:
            y = _swiglu(h, ffn_w_gu[j], ffn_w_down[j])
        else:
            y = _moe(h, moe_w_router[j], moe_w_gu[j], moe_w_down[j])
        y = _rms_norm(y, norm_g[i, 3])
        if with_ctx_out:
            ctx = ctx + cg_f * y[:, :CTX_LEN]
            x = x + g_f * y[:, CTX_LEN:]
        else:
            x = x + g_f * y
    return x
```

```python
import functools
import math

import jax
import jax.numpy as jnp
from jax import lax
from jax.experimental import pallas as pl
from jax.experimental.pallas import tpu as pltpu

D_MODEL = 4096
DEPTH = 4
GRID_W = 64
CTX_LEN = 256
NORM_EPS = 1e-6
S5_GROUP = 16
S5_GROUPS = D_MODEL // S5_GROUP
S5_STATE = 64
NA_HEADS = 32
NA_HEAD_DIM = D_MODEL // NA_HEADS
NA_KH = 8
NA_KW = 16
N_EXPERTS = 8
EXPERT_DIM = 1024

S5_CHUNK = 16
S5_CK = S5_CHUNK * S5_GROUP
S5_SCAN_BLOCK = 16
S5_GROUP_BLOCK = 8
NA_ROWS_PER_STEP = 4
NA_NEG = -1e30
ROW_BLOCK = 256
VMEM_LIMIT_V7X = 56 * 1024 * 1024

BF16 = jnp.bfloat16
F32 = jnp.float32


def _params(*sem):
    return pltpu.CompilerParams(dimension_semantics=sem, vmem_limit_bytes=VMEM_LIMIT_V7X)


def _adaln_body(cond_ref, w_ref, b_ref, o_ref):
    cond = cond_ref[...]
    s = cond * jax.nn.sigmoid(cond)
    w = w_ref[...]
    r0 = jnp.sum(w * s[:, 0:1], axis=0, keepdims=True)
    r1 = jnp.sum(w * s[:, 1:2], axis=0, keepdims=True)
    o_ref[...] = jnp.concatenate([r0, r1], axis=0) + b_ref[...]


def _adaln(cond_t, ada_w, ada_b, tn=512):
    depth, d, n = ada_w.shape
    return pl.pallas_call(
        _adaln_body,
        grid=(depth, n // tn),
        in_specs=[
            pl.BlockSpec((d, 2), lambda l, j: (0, 0)),
            pl.BlockSpec((None, d, tn), lambda l, j: (l, 0, j)),
            pl.BlockSpec((None, 1, tn), lambda l, j: (l, 0, j)),
        ],
        out_specs=pl.BlockSpec((None, 2, tn), lambda l, j: (l, 0, j)),
        out_shape=jax.ShapeDtypeStruct((depth, 2, n), F32),
        compiler_params=_params("parallel", "parallel"),
        name="adaln",
    )(cond_t, ada_w, ada_b.reshape(depth, 1, n))


def _rms(x, g):
    return x * lax.rsqrt(jnp.mean(x * x, axis=-1, keepdims=True) + NORM_EPS) * g


def _route(h, wr_ref, gates_ref):
    logits = jnp.dot(h, wr_ref[...], preferred_element_type=F32, precision=lax.Precision.HIGHEST)
    lane = lax.broadcasted_iota(jnp.int32, logits.shape, 1).astype(F32)
    n_lanes = float(logits.shape[1])
    logits = jnp.where(lane < N_EXPERTS, logits, -jnp.inf)
    m1 = jnp.max(logits, axis=-1, keepdims=True)
    i1 = jnp.min(jnp.where(logits == m1, lane, n_lanes), axis=-1, keepdims=True)
    rest = jnp.where(lane == i1, -jnp.inf, logits)
    m2 = jnp.max(rest, axis=-1, keepdims=True)
    i2 = jnp.min(jnp.where(rest == m2, lane, n_lanes), axis=-1, keepdims=True)
    e2 = jnp.exp(m2 - m1)
    w1 = 1.0 / (1.0 + e2)
    w2 = e2 / (1.0 + e2)
    gates_ref[...] = jnp.where(lane == i1, w1, 0.0) + jnp.where(lane == i2, w2, 0.0)


def _pre_body(x_ref, g_ref, sh_ref, sc_ref, h_ref):
    h = _rms(x_ref[...], g_ref[...]) * (1.0 + sc_ref[...]) + sh_ref[...]
    h_ref[...] = h.astype(h_ref.dtype)


def _post_body(*refs, with_next, with_router):
    y_ref, x_ref, gp_ref, gate_ref = refs[:4]
    x_new = x_ref[...] + gate_ref[...] * _rms(y_ref[...].astype(F32), gp_ref[...])
    if not with_next:
        refs[4][...] = x_new
        return
    gn_ref, sh_ref, sc_ref = refs[4:7]
    rest = refs[7:]
    if with_router:
        wr_ref, rest = rest[0], rest[1:]
    xo_ref, h_ref = rest[:2]
    xo_ref[...] = x_new
    h = _rms(x_new, gn_ref[...]) * (1.0 + sc_ref[...]) + sh_ref[...]
    h_ref[...] = h.astype(h_ref.dtype)
    if with_router:
        _route(h, wr_ref, rest[2])


def _row_spec(d, off=0):
    return pl.BlockSpec((ROW_BLOCK, d), lambda i: (i + off, 0))


def _vec_spec(d):
    return pl.BlockSpec((1, d), lambda i: (0, 0))


def _mod_spec(d, n_ctx_blocks):
    return pl.BlockSpec((None, 1, d), lambda i: (jnp.where(i < n_ctx_blocks, 0, 1), 0, 0))


def _prenorm(x, g, shift, scale):
    m, d = x.shape
    nc = CTX_LEN // ROW_BLOCK
    return pl.pallas_call(
        _pre_body,
        grid=(m // ROW_BLOCK,),
        in_specs=[_row_spec(d), _vec_spec(d), _mod_spec(d, nc), _mod_spec(d, nc)],
        out_specs=_row_spec(d),
        out_shape=jax.ShapeDtypeStruct((m, d), BF16),
        compiler_params=_params("parallel"),
        name="prenorm",
    )(x, g.reshape(1, d), shift, scale)


def _postnorm(y, x, g_post, gate, nxt=None, w_router=None, latent_only=False):
    m, d = x.shape
    nc = CTX_LEN // ROW_BLOCK
    off = nc if latent_only else 0
    n_ctx = 0 if latent_only else nc
    m_out = m - off * ROW_BLOCK
    ins = [y, x, g_post.reshape(1, d), gate]
    in_specs = [_row_spec(d, off), _row_spec(d, off), _vec_spec(d), _mod_spec(d, n_ctx)]
    out_shape = [jax.ShapeDtypeStruct((m_out, d), F32)]
    out_specs = [_row_spec(d)]
    if nxt is not None:
        ins += [nxt[0].reshape(1, d), nxt[1], nxt[2]]
        in_specs += [_vec_spec(d), _mod_spec(d, n_ctx), _mod_spec(d, n_ctx)]
        out_shape.append(jax.ShapeDtypeStruct((m_out, d), BF16))
        out_specs.append(_row_spec(d))
        if w_router is not None:
            lanes = 128
            wr = jnp.zeros((d, lanes), F32).at[:, :N_EXPERTS].set(w_router)
            ins.append(wr)
            in_specs.append(pl.BlockSpec((d, lanes), lambda i: (0, 0)))
            out_shape.append(jax.ShapeDtypeStruct((m_out, lanes), F32))
            out_specs.append(pl.BlockSpec((ROW_BLOCK, lanes), lambda i: (i, 0)))
    out = pl.pallas_call(
        functools.partial(_post_body, with_next=nxt is not None, with_router=w_router is not None),
        grid=(m_out // ROW_BLOCK,),
        in_specs=in_specs,
        out_specs=out_specs,
        out_shape=out_shape,
        compiler_params=_params("parallel"),
        name="postnorm",
    )(*ins)
    return out


def _mm_body(a_ref, w_ref, o_ref, *scratch, nk):
    part = jnp.dot(a_ref[...], w_ref[...], preferred_element_type=F32)
    if nk == 1:
        o_ref[...] = part.astype(o_ref.dtype)
        return
    acc_ref, = scratch
    k = pl.program_id(2)

    @pl.when(k == 0)
    def _():
        acc_ref[...] = part

    @pl.when(jnp.logical_and(k > 0, k < nk - 1))
    def _():
        acc_ref[...] += part

    @pl.when(k == nk - 1)
    def _():
        o_ref[...] = (acc_ref[...] + part).astype(o_ref.dtype)


def _matmul(a, w, *, tm, tn, tk, out_dtype):
    m, kk = a.shape
    n = w.shape[1]
    nk = kk // tk
    return pl.pallas_call(
        functools.partial(_mm_body, nk=nk),
        grid=(m // tm, n // tn, nk),
        in_specs=[pl.BlockSpec((tm, tk), lambda i, j, k: (i, k)),
                  pl.BlockSpec((tk, tn), lambda i, j, k: (k, j))],
        out_specs=pl.BlockSpec((tm, tn), lambda i, j, k: (i, j)),
        out_shape=jax.ShapeDtypeStruct((m, n), out_dtype),
        scratch_shapes=[pltpu.VMEM((tm, tn), F32)] if nk > 1 else [],
        compiler_params=_params("parallel", "parallel", "arbitrary"),
        name="matmul",
    )(a, w)


def _gated_body(*refs, act, with_route):
    a_ref, wg_ref, wu_ref = refs[:3]
    o_ref = refs[-1]
    a = a_ref[...]
    g = jnp.dot(a, wg_ref[...], preferred_element_type=F32)
    u = jnp.dot(a, wu_ref[...], preferred_element_type=F32)
    if act == "swiglu":
        r = g * jax.nn.sigmoid(g) * u
    else:
        r = g * jax.nn.sigmoid(u)
    if with_route:
        r = r * refs[3][...]
    o_ref[...] = r.astype(o_ref.dtype)


def _gated_matmul(a, w, *, act, tm, tn, out_dtype, route=None):
    m, kk = a.shape
    e, _, f2 = w.shape
    nf = f2 // 2 // tn
    ins = [a, w, w]
    in_specs = [pl.BlockSpec((tm, kk), lambda i, x, j: (i, 0)),
                pl.BlockSpec((None, kk, tn), lambda i, x, j: (x, 0, j)),
                pl.BlockSpec((None, kk, tn), lambda i, x, j: (x, 0, nf + j))]
    if route is not None:
        ins.append(route)
        in_specs.append(pl.BlockSpec((None, tm, 1), lambda i, x, j: (x, i, 0)))
    return pl.pallas_call(
        functools.partial(_gated_body, act=act, with_route=route is not None),
        grid=(m // tm, e, nf),
        in_specs=in_specs,
        out_specs=pl.BlockSpec((tm, tn), lambda i, x, j: (i, x * nf + j)),
        out_shape=jax.ShapeDtypeStruct((m, e * f2 // 2), out_dtype),
        compiler_params=_params("parallel", "parallel", "parallel"),
        name="gated_matmul",
    )(*ins)


def _s5_lag_body(b_ref, c_ref, o_ref):
    for g in range(b_ref.shape[0]):
        o_ref[g] = jnp.dot(b_ref[g], c_ref[g], preferred_element_type=F32,
                           precision=lax.Precision.HIGHEST)


def _s5_lag_kernels(b_lag, c_cat):
    g, r, n2 = b_lag.shape
    gb = S5_GROUP_BLOCK
    return pl.pallas_call(
        _s5_lag_body,
        grid=(g // gb,),
        in_specs=[pl.BlockSpec((gb, r, n2), lambda i: (i, 0, 0)),
                  pl.BlockSpec((gb, n2, S5_GROUP), lambda i: (i, 0, 0))],
        out_specs=pl.BlockSpec((gb, r, S5_GROUP), lambda i: (i, 0, 0)),
        out_shape=jax.ShapeDtypeStruct((g, r, S5_GROUP), F32),
        compiler_params=_params("parallel"),
        name="s5_lag_kernels",
    )(b_lag, c_cat)


def _s5_in_body(u_ref, b_ref, v_ref):
    for g in range(u_ref.shape[0]):
        v_ref[g] = jnp.dot(u_ref[g], b_ref[g], preferred_element_type=F32)


def _s5_state_inputs(u, bc):
    g, nc, ck = u.shape
    gb = S5_GROUP_BLOCK
    return pl.pallas_call(
        _s5_in_body,
        grid=(g // gb,),
        in_specs=[pl.BlockSpec((gb, nc, ck), lambda i: (i, 0, 0)),
                  pl.BlockSpec((gb, ck, 4 * S5_STATE), lambda i: (i, 0, 0))],
        out_specs=pl.BlockSpec((gb, nc, 4 * S5_STATE), lambda i: (i, 0, 0)),
        out_shape=jax.ShapeDtypeStruct((g, nc, 4 * S5_STATE), F32),
        compiler_params=_params("parallel"),
        name="s5_state_inputs",
    )(u, bc)


def _s5_scan_body(vf_ref, vr_ref, af_ref, ar_ref, sf_ref, sr_ref, stf_ref, str_ref):
    half = stf_ref.shape[-1]

    @pl.when(pl.program_id(0) == 0)
    def _():
        stf_ref[...] = jnp.zeros_like(stf_ref)
        str_ref[...] = jnp.zeros_like(str_ref)

    def run(v_ref, a_ref, s_ref, st_ref, order):
        ar, ai = a_ref[:, :half], a_ref[:, half:]
        sr, si = st_ref[0], st_ref[1]
        for j in order:
            s_ref[j, :, :half] = sr.astype(s_ref.dtype)
            s_ref[j, :, half:] = si.astype(s_ref.dtype)
            vr, vi = v_ref[j, :, :half], v_ref[j, :, half:]
            sr, si = ar * sr - ai * si + vr, ar * si + ai * sr + vi
        st_ref[0] = sr
        st_ref[1] = si

    nb = S5_SCAN_BLOCK
    run(vf_ref, af_ref, sf_ref, stf_ref, range(nb))
    run(vr_ref, ar_ref, sr_ref, str_ref, range(nb - 1, -1, -1))


def _s5_scan(vt, a_f, a_r):
    nc, gq, w4 = vt.shape
    w = w4 // 2
    nb = S5_SCAN_BLOCK
    nblk = nc // nb
    nctx = CTX_LEN // S5_CHUNK // nb
    assert nctx == 1

    def rev(t):
        return jnp.where(t < nctx, 0, nblk - t)

    return pl.pallas_call(
        _s5_scan_body,
        grid=(nblk,),
        in_specs=[pl.BlockSpec((nb, gq, w), lambda t: (t, 0, 0)),
                  pl.BlockSpec((nb, gq, w), lambda t: (rev(t), 0, 1)),
                  pl.BlockSpec((gq, w), lambda t: (0, 0)),
                  pl.BlockSpec((gq, w), lambda t: (0, 0))],
        out_specs=[pl.BlockSpec((nb, gq, w), lambda t: (t, 0, 0)),
                   pl.BlockSpec((nb, gq, w), lambda t: (rev(t), 0, 0))],
        out_shape=[jax.ShapeDtypeStruct((nc, gq, w), BF16)] * 2,
        scratch_shapes=[pltpu.VMEM((2, gq, w // 2), F32)] * 2,
        compiler_params=_params("arbitrary"),
        name="s5_scan",
    )(vt, vt, a_f, a_r)


def _s5_out_body(u_ref, s_ref, t_ref, c_ref, y_ref):
    for g in range(u_ref.shape[0]):
        y = (jnp.dot(u_ref[g], t_ref[g], preferred_element_type=F32)
             + jnp.dot(s_ref[g], c_ref[g], preferred_element_type=F32))
        y_ref[g] = jax.nn.gelu(y).astype(y_ref.dtype)


def _s5_outputs(u, s, toep, cc):
    g, nc, ck = u.shape
    gb = S5_GROUP_BLOCK
    blk = lambda a, b: pl.BlockSpec((gb, a, b), lambda i: (i, 0, 0))
    return pl.pallas_call(
        _s5_out_body,
        grid=(g // gb,),
        in_specs=[blk(nc, ck), blk(nc, 4 * S5_STATE), blk(ck, ck), blk(4 * S5_STATE, ck)],
        out_specs=blk(nc, ck),
        out_shape=jax.ShapeDtypeStruct((g, nc, ck), BF16),
        compiler_params=_params("parallel"),
        name="s5_outputs",
    )(u, s, toep, cc)


def _s5_operators(a_re, a_im, log_dt, b_re, b_im, c_re, c_im, d_skip):
    L, G, N, C = S5_CHUNK, S5_GROUPS, S5_STATE, S5_GROUP
    lam = lax.complex(a_re.astype(F32), a_im.astype(F32))
    dt = jnp.exp(log_dt.astype(F32))[..., None]
    lam_dt = lam * dt
    steps = jnp.arange(L + 1, dtype=F32)[:, None, None, None]
    a_pow = jnp.exp(lam_dt[None] * steps)
    b_bar = ((a_pow[1] - 1.0) / lam)[..., None] * lax.complex(b_re.astype(F32), b_im.astype(F32))
    c_mat = lax.complex(c_re.astype(F32), c_im.astype(F32))

    b_lag = a_pow[:L, :, :, :, None] * b_bar[None]
    b_lag = jnp.transpose(b_lag, (1, 2, 0, 4, 3)).reshape(2 * G, L * C, N)
    b_lag = jnp.concatenate([b_lag.real, -b_lag.imag], axis=-1)
    c_t = jnp.swapaxes(c_mat, -1, -2).reshape(2 * G, N, C)
    c_cat = jnp.concatenate([c_t.real, c_t.imag], axis=1)
    k_lag = _s5_lag_kernels(b_lag, c_cat).reshape(2, G, L, C, C)

    k_idx = jnp.arange(L)
    lag = k_idx[None, :] - k_idx[:, None]
    kf = jnp.where((lag >= 0)[None, :, :, None, None], k_lag[0][:, jnp.clip(lag, 0, L - 1)], 0.0)
    kr = jnp.where((lag <= 0)[None, :, :, None, None], k_lag[1][:, jnp.clip(-lag, 0, L - 1)], 0.0)
    eye_k = jnp.eye(L, dtype=F32)[None, :, :, None, None]
    d_diag = d_skip.astype(F32).reshape(G, 1, 1, C, 1) * jnp.eye(C, dtype=F32)[None, None, None]
    toep = kf + kr + eye_k * d_diag
    toep = jnp.transpose(toep, (0, 1, 3, 2, 4)).reshape(G, L * C, L * C)

    wf = a_pow[L - 1 - k_idx, 0]
    wb = a_pow[k_idx, 1]
    bc_f = wf[..., None] * b_bar[0][None]
    bc_b = wb[..., None] * b_bar[1][None]
    to_rows = lambda z: jnp.transpose(z, (1, 0, 3, 2)).reshape(G, L * C, N)
    bc = jnp.concatenate([to_rows(bc_f.real), to_rows(bc_f.imag),
                          to_rows(bc_b.real), to_rows(bc_b.imag)], axis=-1)

    ca_f = c_mat[0][:, None] * a_pow[k_idx + 1, 0].transpose(1, 0, 2)[:, :, None, :]
    ca_b = c_mat[1][:, None] * a_pow[L - k_idx, 1].transpose(1, 0, 2)[:, :, None, :]
    to_cols = lambda z: jnp.transpose(z, (0, 3, 1, 2)).reshape(G, N, L * C)
    cc = jnp.concatenate([to_cols(ca_f.real), -to_cols(ca_f.imag),
                          to_cols(ca_b.real), -to_cols(ca_b.imag)], axis=1)

    def pair(z):
        return jnp.transpose(z.reshape(2, G // 2, N), (1, 0, 2)).reshape(G // 2, 2 * N)
    a_chunk = a_pow[L]
    a_f = jnp.concatenate([pair(a_chunk[0].real), pair(a_chunk[0].imag)], axis=-1)
    a_r = jnp.concatenate([pair(a_chunk[1].real), pair(a_chunk[1].imag)], axis=-1)
    return toep.astype(BF16), bc.astype(BF16), cc.astype(BF16), a_f, a_r


def _s5_mixer(h, ops):
    toep, bc, cc, a_f, a_r = ops
    m, d = h.shape
    L, G, N, C = S5_CHUNK, S5_GROUPS, S5_STATE, S5_GROUP
    nc = m // L
    u = jnp.transpose(h.reshape(nc, L, G, C), (2, 0, 1, 3)).reshape(G, nc, L * C)
    v = _s5_state_inputs(u, bc)
    vt = jnp.transpose(v.reshape(2, G // 2, nc, 4, N), (2, 1, 3, 0, 4)).reshape(nc, G // 2, 8 * N)
    s_f, s_r = _s5_scan(vt, a_f, a_r)
    s = jnp.stack([s_f.reshape(nc, G // 2, 2, 2, N), s_r.reshape(nc, G // 2, 2, 2, N)], axis=2)
    s = jnp.transpose(s, (4, 1, 0, 2, 3, 5)).reshape(G, nc, 4 * N)
    y = _s5_outputs(u, s, toep, cc)
    return jnp.transpose(y.reshape(G, nc, L, C), (1, 2, 0, 3)).reshape(m, d)


def _na_bias_table(rpb):
    var = jnp.arange(NA_KH)
    row_rel = jnp.arange(NA_KH)[None, :] - var[:, None] + (NA_KH - 1)
    q = jnp.arange(GRID_W)
    col_start = jnp.clip(q - NA_KW // 2, 0, GRID_W - NA_KW)
    kc = jnp.arange(GRID_W)
    inside = (kc[None, :] >= col_start[:, None]) & (kc[None, :] < col_start[:, None] + NA_KW)
    col_rel = jnp.clip(kc[None, :] - q[:, None] + (NA_KW - 1), 0, 2 * NA_KW - 2)
    t = rpb.astype(F32)[:, row_rel][:, :, :, col_rel]
    t = jnp.where(inside[None, None, None], t, NA_NEG)
    return jnp.transpose(t, (1, 0, 3, 2, 4)).reshape(NA_KH, rpb.shape[0], GRID_W, NA_KH * GRID_W)


def _na_body(q_ref, k_ref, v_ref, bias_ref, o_ref, *, rows):
    i = pl.program_id(1)
    scale = NA_HEAD_DIM ** -0.5
    dn = (((1,), (1,)), ((), ()))
    kc = k_ref[0:CTX_LEN, :]
    vc = v_ref[0:CTX_LEN, :]
    nctx = CTX_LEN // (NA_ROWS_PER_STEP * GRID_W)
    band = NA_KH * GRID_W

    @pl.when(i < nctx)
    def _():
        q = q_ref[...]
        s = lax.dot_general(q, kc, dn, preferred_element_type=F32) * scale
        p = jnp.exp(s - jnp.max(s, axis=-1, keepdims=True))
        o = jnp.dot(p.astype(BF16), vc, preferred_element_type=F32)
        o_ref[...] = (o / jnp.sum(p, axis=-1, keepdims=True)).astype(o_ref.dtype)

    @pl.when(i >= nctx)
    def _():
        for sub in range(NA_ROWS_PER_STEP):
            r = (i - nctx) * NA_ROWS_PER_STEP + sub
            r0 = jnp.clip(r - NA_KH // 2, 0, rows - NA_KH)
            start = pl.multiple_of(CTX_LEN + r0 * GRID_W, GRID_W)
            q = q_ref[sub * GRID_W:(sub + 1) * GRID_W, :]
            kb = k_ref[pl.ds(start, band), :]
            vb = v_ref[pl.ds(start, band), :]
            s_w = lax.dot_general(q, kb, dn, preferred_element_type=F32) * scale + bias_ref[r - r0]
            s_c = lax.dot_general(q, kc, dn, preferred_element_type=F32) * scale
            mx = jnp.maximum(jnp.max(s_w, axis=-1, keepdims=True), jnp.max(s_c, axis=-1, keepdims=True))
            p_w = jnp.exp(s_w - mx)
            p_c = jnp.exp(s_c - mx)
            den = jnp.sum(p_w, axis=-1, keepdims=True) + jnp.sum(p_c, axis=-1, keepdims=True)
            o = (jnp.dot(p_w.astype(BF16), vb, preferred_element_type=F32)
                 + jnp.dot(p_c.astype(BF16), vc, preferred_element_type=F32))
            o_ref[sub * GRID_W:(sub + 1) * GRID_W, :] = (o / den).astype(o_ref.dtype)


def _na_attention(qkv, bias):
    m = qkv.shape[0]
    rows = (m - CTX_LEN) // GRID_W
    tq = NA_ROWS_PER_STEP * GRID_W
    dh = NA_HEAD_DIM
    return pl.pallas_call(
        functools.partial(_na_body, rows=rows),
        grid=(NA_HEADS, m // tq),
        in_specs=[pl.BlockSpec((tq, dh), lambda h, i: (i, h)),
                  pl.BlockSpec((m, dh), lambda h, i: (0, NA_HEADS + h)),
                  pl.BlockSpec((m, dh), lambda h, i: (0, 2 * NA_HEADS + h)),
                  pl.BlockSpec((NA_KH, None, GRID_W, NA_KH * GRID_W), lambda h, i: (0, h, 0, 0))],
        out_specs=pl.BlockSpec((tq, dh), lambda h, i: (i, h)),
        out_shape=jax.ShapeDtypeStruct((m, D_MODEL), BF16),
        compiler_params=_params("parallel", "arbitrary"),
        name="na_attention",
    )(qkv, qkv, qkv, bias)


def _tiles(m):
    for blocks in range(5, 0, -1):
        if m % (blocks * ROW_BLOCK) == 0:
            return blocks * ROW_BLOCK
    raise ValueError(m)


def kernel(x, c, ctx, c_ctx, ada_w, ada_b, norm_g, s5_a_re, s5_a_im, s5_log_dt, s5_b_re, s5_b_im,
           s5_c_re, s5_c_im, s5_d, s5_w_glu, na_w_qkv, na_w_o, na_rpb, ffn_w_gu, ffn_w_down,
           moe_w_router, moe_w_gu, moe_w_down):
    assert x.shape[0] == 1 and x.shape[2] == D_MODEL and ctx.shape[1] == CTX_LEN
    d = D_MODEL
    xc = jnp.concatenate([ctx[0], x[0]], axis=0).astype(F32)
    m = xc.shape[0]
    tm = _tiles(m)

    cond_t = jnp.stack([c_ctx, c[0]], axis=1).astype(F32)
    mods = _adaln(cond_t, ada_w.astype(F32), ada_b.astype(F32))
    mods = mods.reshape(DEPTH, 2, 6, 1, d)
    mod = lambda i, k: mods[i, :, k]

    h = _prenorm(xc, norm_g[0, 0], mod(0, 0), mod(0, 1))
    for i in range(DEPTH):
        j = i // 2
        last = i == DEPTH - 1
        nxt_f = (norm_g[i, 2], mod(i, 3), mod(i, 4))
        if i % 2 == 0:
            ops = _s5_operators(s5_a_re[j], s5_a_im[j], s5_log_dt[j], s5_b_re[j], s5_b_im[j],
                                s5_c_re[j], s5_c_im[j], s5_d[j])
            g = _s5_mixer(h, ops)
            y = _gated_matmul(g, s5_w_glu[j].astype(BF16)[None], act="glu", tm=tm, tn=512, out_dtype=F32)
            xc, h = _postnorm(y, xc, norm_g[i, 1], mod(i, 2), nxt_f)
            a = _gated_matmul(h, ffn_w_gu[j].astype(BF16)[None], act="swiglu", tm=tm, tn=512,
                              out_dtype=BF16)
            y = _matmul(a, ffn_w_down[j].astype(BF16), tm=tm, tn=512, tk=a.shape[1], out_dtype=F32)
        else:
            qkv = _matmul(h, na_w_qkv[j].astype(BF16), tm=tm, tn=1024, tk=d, out_dtype=BF16)
            o = _na_attention(qkv, _na_bias_table(na_rpb[j]))
            y = _matmul(o, na_w_o[j].astype(BF16), tm=tm, tn=512, tk=d, out_dtype=F32)
            xc, h, gates = _postnorm(y, xc, norm_g[i, 1], mod(i, 2), nxt_f, w_router=moe_w_router[j])
            route = jnp.transpose(gates[:, :N_EXPERTS])[:, :, None]
            a = _gated_matmul(h, moe_w_gu[j].astype(BF16), act="swiglu", tm=tm, tn=512,
                              out_dtype=BF16, route=route)
            w_down = moe_w_down[j].astype(BF16).reshape(N_EXPERTS * EXPERT_DIM, d)
            y = _matmul(a, w_down, tm=tm, tn=512, tk=4096, out_dtype=F32)
        if last:
            out, = _postnorm(y, xc, norm_g[i, 3], mod(i, 5), latent_only=True)
            return out[None].astype(x.dtype)
        nxt_m = (norm_g[i + 1, 0], mod(i + 1, 0), mod(i + 1, 1))
        xc, h = _postnorm(y, xc, norm_g[i, 3], mod(i, 5), nxt_m)
```

```python
import functools

import jax
import jax.numpy as jnp
from jax import lax
from jax.experimental import pallas as pl
from jax.experimental.pallas import tpu as pltpu

D_MODEL = 4096
DEPTH = 4
GRID_W = 64
CTX_LEN = 256
NORM_EPS = 1e-6
S5_GROUP = 16
S5_GROUPS = D_MODEL // S5_GROUP
S5_STATE = 64
NA_HEADS = 32
NA_HEAD_DIM = D_MODEL // NA_HEADS
NA_KH = 8
NA_KW = 16
N_EXPERTS = 8
EXPERT_DIM = 1024

LANES = 128
S5_CHUNK = 16
S5_CK = S5_CHUNK * S5_GROUP
S5_SLAB_GROUPS = LANES // S5_GROUP
S5_SLABS = S5_GROUPS // S5_SLAB_GROUPS
S5_SLAB_W = S5_CHUNK * LANES
S5_CHUNK_TILE = 520
NA_ROWS_PER_STEP = 4
NA_HEADS_PER_STEP = 2
NA_BAND = NA_KH + NA_ROWS_PER_STEP
NA_NEG = -1e30
ROW_BLOCK = 256
VMEM_LIMIT_V7X = 56 * 1024 * 1024

BF16 = jnp.bfloat16
F32 = jnp.float32


def _params(*sem):
    return pltpu.CompilerParams(dimension_semantics=sem, vmem_limit_bytes=VMEM_LIMIT_V7X)


def _adaln_body(cond_ref, w_ref, b_ref, o_ref):
    cond = cond_ref[...]
    s = cond * jax.nn.sigmoid(cond)
    w = w_ref[...]
    r0 = jnp.sum(w * s[:, 0:1], axis=0, keepdims=True)
    r1 = jnp.sum(w * s[:, 1:2], axis=0, keepdims=True)
    o_ref[...] = jnp.concatenate([r0, r1], axis=0) + b_ref[...]


def _adaln(cond_t, ada_w, ada_b, tn=512):
    depth, d, n = ada_w.shape
    return pl.pallas_call(
        _adaln_body,
        grid=(depth, n // tn),
        in_specs=[
            pl.BlockSpec((d, 2), lambda l, j: (0, 0)),
            pl.BlockSpec((None, d, tn), lambda l, j: (l, 0, j)),
            pl.BlockSpec((None, 1, tn), lambda l, j: (l, 0, j)),
        ],
        out_specs=pl.BlockSpec((None, 2, tn), lambda l, j: (l, 0, j)),
        out_shape=jax.ShapeDtypeStruct((depth, 2, n), F32),
        compiler_params=_params("parallel", "parallel"),
        name="adaln",
    )(cond_t, ada_w, ada_b.reshape(depth, 1, n))


def _rms(x, g):
    return x * lax.rsqrt(jnp.mean(x * x, axis=-1, keepdims=True) + NORM_EPS) * g


def _route(h, wr_ref, gates_ref):
    logits = jnp.dot(h, wr_ref[...], preferred_element_type=F32, precision=lax.Precision.HIGHEST)
    lane = lax.broadcasted_iota(jnp.int32, logits.shape, 1).astype(F32)
    n_lanes = float(logits.shape[1])
    logits = jnp.where(lane < N_EXPERTS, logits, -jnp.inf)
    m1 = jnp.max(logits, axis=-1, keepdims=True)
    i1 = jnp.min(jnp.where(logits == m1, lane, n_lanes), axis=-1, keepdims=True)
    rest = jnp.where(lane == i1, -jnp.inf, logits)
    m2 = jnp.max(rest, axis=-1, keepdims=True)
    i2 = jnp.min(jnp.where(rest == m2, lane, n_lanes), axis=-1, keepdims=True)
    e2 = jnp.exp(m2 - m1)
    w1 = 1.0 / (1.0 + e2)
    w2 = e2 / (1.0 + e2)
    gates_ref[...] = jnp.where(lane == i1, w1, 0.0) + jnp.where(lane == i2, w2, 0.0)


def _pre_body(x_ref, g_ref, sh_ref, sc_ref, h_ref):
    h = _rms(x_ref[...], g_ref[...]) * (1.0 + sc_ref[...]) + sh_ref[...]
    h_ref[...] = h.astype(h_ref.dtype)


def _post_body(*refs, with_next, with_router):
    y_ref, x_ref, gp_ref, gate_ref = refs[:4]
    x_new = x_ref[...] + gate_ref[...] * _rms(y_ref[...].astype(F32), gp_ref[...])
    if not with_next:
        refs[4][...] = x_new
        return
    gn_ref, sh_ref, sc_ref = refs[4:7]
    rest = refs[7:]
    if with_router:
        wr_ref, rest = rest[0], rest[1:]
    xo_ref, h_ref = rest[:2]
    xo_ref[...] = x_new
    h = _rms(x_new, gn_ref[...]) * (1.0 + sc_ref[...]) + sh_ref[...]
    h_ref[...] = h.astype(h_ref.dtype)
    if with_router:
        _route(h, wr_ref, rest[2])


def _row_spec(d, off=0):
    return pl.BlockSpec((ROW_BLOCK, d), lambda i: (i + off, 0))


def _vec_spec(d):
    return pl.BlockSpec((1, d), lambda i: (0, 0))


def _mod_spec(d, n_ctx_blocks):
    return pl.BlockSpec((None, 1, d), lambda i: (jnp.where(i < n_ctx_blocks, 0, 1), 0, 0))


def _prenorm(x, g, shift, scale):
    m, d = x.shape
    nc = CTX_LEN // ROW_BLOCK
    return pl.pallas_call(
        _pre_body,
        grid=(m // ROW_BLOCK,),
        in_specs=[_row_spec(d), _vec_spec(d), _mod_spec(d, nc), _mod_spec(d, nc)],
        out_specs=_row_spec(d),
        out_shape=jax.ShapeDtypeStruct((m, d), BF16),
        compiler_params=_params("parallel"),
        name="prenorm",
    )(x, g.reshape(1, d), shift, scale)


def _postnorm(y, x, g_post, gate, nxt=None, w_router=None, latent_only=False):
    m, d = x.shape
    nc = CTX_LEN // ROW_BLOCK
    off = nc if latent_only else 0
    n_ctx = 0 if latent_only else nc
    m_out = m - off * ROW_BLOCK
    ins = [y, x, g_post.reshape(1, d), gate]
    in_specs = [_row_spec(d, off), _row_spec(d, off), _vec_spec(d), _mod_spec(d, n_ctx)]
    out_shape = [jax.ShapeDtypeStruct((m_out, d), F32)]
    out_specs = [_row_spec(d)]
    if nxt is not None:
        ins += [nxt[0].reshape(1, d), nxt[1], nxt[2]]
        in_specs += [_vec_spec(d), _mod_spec(d, n_ctx), _mod_spec(d, n_ctx)]
        out_shape.append(jax.ShapeDtypeStruct((m_out, d), BF16))
        out_specs.append(_row_spec(d))
        if w_router is not None:
            wr = jnp.zeros((d, LANES), F32).at[:, :N_EXPERTS].set(w_router)
            ins.append(wr)
            in_specs.append(pl.BlockSpec((d, LANES), lambda i: (0, 0)))
            out_shape.append(jax.ShapeDtypeStruct((m_out, LANES), F32))
            out_specs.append(pl.BlockSpec((ROW_BLOCK, LANES), lambda i: (i, 0)))
    out = pl.pallas_call(
        functools.partial(_post_body, with_next=nxt is not None, with_router=w_router is not None),
        grid=(m_out // ROW_BLOCK,),
        in_specs=in_specs,
        out_specs=out_specs,
        out_shape=out_shape,
        compiler_params=_params("parallel"),
        name="postnorm",
    )(*ins)
    return out


def _mm_body(a_ref, w_ref, o_ref):
    o_ref[...] = jnp.dot(a_ref[...], w_ref[...], preferred_element_type=F32).astype(o_ref.dtype)


def _matmul(a, w, *, tm, tn, out_dtype):
    m, kk = a.shape
    n = w.shape[1]
    return pl.pallas_call(
        _mm_body,
        grid=(m // tm, n // tn),
        in_specs=[pl.BlockSpec((tm, kk), lambda i, j: (i, 0)),
                  pl.BlockSpec((kk, tn), lambda i, j: (0, j))],
        out_specs=pl.BlockSpec((tm, tn), lambda i, j: (i, j)),
        out_shape=jax.ShapeDtypeStruct((m, n), out_dtype),
        compiler_params=_params("parallel", "parallel"),
        name="matmul",
    )(a, w)


def _gated_body(*refs, act, with_route):
    a_ref, wg_ref, wu_ref = refs[:3]
    o_ref = refs[-1]
    a = a_ref[...]
    g = jnp.dot(a, wg_ref[...], preferred_element_type=F32)
    u = jnp.dot(a, wu_ref[...], preferred_element_type=F32)
    if act == "swiglu":
        r = g * jax.nn.sigmoid(g) * u
    else:
        r = g * jax.nn.sigmoid(u)
    if with_route:
        r = r * refs[3][...]
    o_ref[...] = r.astype(o_ref.dtype)


def _gated_matmul(a, w, *, act, tm, tn, out_dtype, route=None):
    m, kk = a.shape
    e, _, f2 = w.shape
    nf = f2 // 2 // tn
    ins = [a, w, w]
    in_specs = [pl.BlockSpec((tm, kk), lambda i, x, j: (i, 0)),
                pl.BlockSpec((None, kk, tn), lambda i, x, j: (x, 0, j)),
                pl.BlockSpec((None, kk, tn), lambda i, x, j: (x, 0, nf + j))]
    if route is not None:
        ins.append(route)
        in_specs.append(pl.BlockSpec((None, tm, 1), lambda i, x, j: (x, i, 0)))
    return pl.pallas_call(
        functools.partial(_gated_body, act=act, with_route=route is not None),
        grid=(m // tm, e, nf),
        in_specs=in_specs,
        out_specs=pl.BlockSpec((tm, tn), lambda i, x, j: (i, x * nf + j)),
        out_shape=jax.ShapeDtypeStruct((m, e * f2 // 2), out_dtype),
        compiler_params=_params("parallel", "parallel", "parallel"),
        name="gated_matmul",
    )(*ins)


def _s5_lag_body(b_ref, c_ref, o_ref):
    for g in range(b_ref.shape[0]):
        o_ref[g] = jnp.dot(b_ref[g], c_ref[g], preferred_element_type=F32,
                           precision=lax.Precision.HIGHEST)


def _s5_lag_kernels(b_lag, c_cat):
    g, r, n2 = b_lag.shape
    gb = S5_SLAB_GROUPS
    return pl.pallas_call(
        _s5_lag_body,
        grid=(g // gb,),
        in_specs=[pl.BlockSpec((gb, r, n2), lambda i: (i, 0, 0)),
                  pl.BlockSpec((gb, n2, S5_GROUP), lambda i: (i, 0, 0))],
        out_specs=pl.BlockSpec((gb, r, S5_GROUP), lambda i: (i, 0, 0)),
        out_shape=jax.ShapeDtypeStruct((g, r, S5_GROUP), F32),
        compiler_params=_params("parallel"),
        name="s5_lag_kernels",
    )(b_lag, c_cat)


def _chunk_tile(nc):
    return S5_CHUNK_TILE if nc % S5_CHUNK_TILE == 0 else nc


def _slab_rows(g, k):
    return pl.ds(k * LANES + g * S5_GROUP, S5_GROUP)


def _s5_in_body(u_ref, bc_ref, vre_ref, vim_ref, w_ref, *, tc):
    first = jnp.logical_and(pl.program_id(0) == 0, pl.program_id(1) == 0)

    @pl.when(first)
    def _():
        w_ref[...] = jnp.zeros_like(w_ref)

    @pl.when(pl.program_id(1) == 0)
    def _():
        for g in range(S5_SLAB_GROUPS):
            for k in range(S5_CHUNK):
                w_ref[g, _slab_rows(g, k), :] = bc_ref[g, k * S5_GROUP:(k + 1) * S5_GROUP, :]

    u = u_ref[...]
    for g in range(S5_SLAB_GROUPS):
        v = jnp.dot(u, w_ref[g], preferred_element_type=F32)
        rows = pl.ds(g, tc, stride=S5_SLAB_GROUPS)
        vre_ref[rows, :] = v[:, :LANES]
        vim_ref[rows, :] = v[:, LANES:]


def _s5_state_inputs(u, bc):
    ns, nc, sw = u.shape
    tc = _chunk_tile(nc)
    gb = S5_SLAB_GROUPS
    n4 = 4 * S5_STATE
    out = pl.BlockSpec((None, tc * gb, LANES), lambda s, m: (s, m, 0))
    return pl.pallas_call(
        functools.partial(_s5_in_body, tc=tc),
        grid=(ns, nc // tc),
        in_specs=[pl.BlockSpec((None, tc, sw), lambda s, m: (s, m, 0)),
                  pl.BlockSpec((gb, S5_CK, n4), lambda s, m: (s, 0, 0))],
        out_specs=[out, out],
        out_shape=[jax.ShapeDtypeStruct((ns, nc * gb, LANES), F32)] * 2,
        scratch_shapes=[pltpu.VMEM((gb, sw, n4), BF16)],
        compiler_params=_params("arbitrary", "arbitrary"),
        name="s5_state_inputs",
    )(u, bc)


def _s5_scan_body(vre_ref, vim_ref, a_ref, sre_ref, sim_ref, *, nc, nctx):
    gb = S5_SLAB_GROUPS
    a_re, a_im = a_ref[:, :LANES], a_ref[:, LANES:]
    fwd = lax.broadcasted_iota(jnp.int32, (gb, LANES), 1) < S5_STATE
    bwd = jnp.logical_not(fwd)

    def step(t, carry):
        s_re, s_im = carry
        t_b = jnp.where(t < nctx, nctx - 1 - t, nc - 1 - (t - nctx))
        rf = pl.ds(pl.multiple_of(t * gb, gb), gb)
        rb = pl.ds(pl.multiple_of(t_b * gb, gb), gb)
        pltpu.store(sre_ref.at[rf, :], s_re, mask=fwd)
        pltpu.store(sre_ref.at[rb, :], s_re, mask=bwd)
        pltpu.store(sim_ref.at[rf, :], s_im, mask=fwd)
        pltpu.store(sim_ref.at[rb, :], s_im, mask=bwd)
        v_re = jnp.where(fwd, vre_ref[rf, :], vre_ref[rb, :])
        v_im = jnp.where(fwd, vim_ref[rf, :], vim_ref[rb, :])
        return a_re * s_re - a_im * s_im + v_re, a_re * s_im + a_im * s_re + v_im

    zero = jnp.zeros((gb, LANES), F32)
    lax.fori_loop(0, nc, step, (zero, zero))


def _s5_scan(v_re, v_im, a_chunk):
    ns, rows, _ = v_re.shape
    gb = S5_SLAB_GROUPS
    nc = rows // gb
    blk = pl.BlockSpec((None, rows, LANES), lambda s: (s, 0, 0))
    return pl.pallas_call(
        functools.partial(_s5_scan_body, nc=nc, nctx=CTX_LEN // S5_CHUNK),
        grid=(ns,),
        in_specs=[blk, blk, pl.BlockSpec((None, gb, 2 * LANES), lambda s: (s, 0, 0))],
        out_specs=[blk, blk],
        out_shape=[jax.ShapeDtypeStruct((ns, rows, LANES), F32)] * 2,
        compiler_params=_params("parallel"),
        name="s5_scan",
    )(v_re, v_im, a_chunk)


def _s5_out_body(u_ref, sre_ref, sim_ref, t_ref, c_ref, y_ref, wt_ref, wc_ref, scat_ref, *, tc):
    gb = S5_SLAB_GROUPS
    n4 = 4 * S5_STATE

    @pl.when(pl.program_id(1) == 0)
    def _():
        row = lax.broadcasted_iota(jnp.int32, (S5_CK, S5_SLAB_W), 0)
        col = lax.broadcasted_iota(jnp.int32, (S5_CK, S5_SLAB_W), 1)
        spread = (row // S5_GROUP) * LANES + row % S5_GROUP
        for g in range(gb):
            place = (col == spread + g * S5_GROUP).astype(BF16)
            tg = jnp.dot(t_ref[g], place, preferred_element_type=F32).astype(BF16)
            for k in range(S5_CHUNK):
                wt_ref[_slab_rows(g, k), :] = tg[k * S5_GROUP:(k + 1) * S5_GROUP, :]
            wc_ref[g * n4:(g + 1) * n4, :] = jnp.dot(c_ref[g], place, preferred_element_type=F32).astype(BF16)

    for g in range(gb):
        rows = pl.ds(g, tc, stride=gb)
        scat_ref[:, g * n4:g * n4 + LANES] = sre_ref[rows, :].astype(BF16)
        scat_ref[:, g * n4 + LANES:(g + 1) * n4] = sim_ref[rows, :].astype(BF16)
    y = (jnp.dot(u_ref[...], wt_ref[...], preferred_element_type=F32)
         + jnp.dot(scat_ref[...], wc_ref[...], preferred_element_type=F32))
    y_ref[...] = jax.nn.gelu(y).astype(y_ref.dtype)


def _s5_outputs(u, s_re, s_im, toep, cc):
    ns, nc, sw = u.shape
    tc = _chunk_tile(nc)
    gb = S5_SLAB_GROUPS
    n4 = 4 * S5_STATE
    state = pl.BlockSpec((None, tc * gb, LANES), lambda i, m: (i, m, 0))
    return pl.pallas_call(
        functools.partial(_s5_out_body, tc=tc),
        grid=(ns, nc // tc),
        in_specs=[pl.BlockSpec((None, tc, sw), lambda i, m: (i, m, 0)), state, state,
                  pl.BlockSpec((gb, S5_CK, S5_CK), lambda i, m: (i, 0, 0)),
                  pl.BlockSpec((gb, n4, S5_CK), lambda i, m: (i, 0, 0))],
        out_specs=pl.BlockSpec((None, tc, sw), lambda i, m: (i, m, 0)),
        out_shape=jax.ShapeDtypeStruct((ns, nc, sw), BF16),
        scratch_shapes=[pltpu.VMEM((sw, sw), BF16), pltpu.VMEM((gb * n4, sw), BF16),
                        pltpu.VMEM((tc, gb * n4), BF16)],
        compiler_params=_params("parallel", "arbitrary"),
        name="s5_outputs",
    )(u, s_re, s_im, toep, cc)


def _s5_operators(a_re, a_im, log_dt, b_re, b_im, c_re, c_im, d_skip):
    L, G, N, C = S5_CHUNK, S5_GROUPS, S5_STATE, S5_GROUP
    a_re, a_im = a_re.astype(F32), a_im.astype(F32)
    b_re, b_im = b_re.astype(F32), b_im.astype(F32)
    c_re, c_im = c_re.astype(F32), c_im.astype(F32)
    dt = jnp.exp(log_dt.astype(F32))[..., None]
    steps = jnp.arange(L + 1, dtype=F32)[:, None, None, None]
    mag = jnp.exp((a_re * dt)[None] * steps)
    ang = (a_im * dt)[None] * steps
    pw_re, pw_im = mag * jnp.cos(ang), mag * jnp.sin(ang)
    p, q = pw_re[1] - 1.0, pw_im[1]
    den = a_re * a_re + a_im * a_im
    f_re, f_im = (p * a_re + q * a_im) / den, (q * a_re - p * a_im) / den
    bb_re = f_re[..., None] * b_re - f_im[..., None] * b_im
    bb_im = f_re[..., None] * b_im + f_im[..., None] * b_re

    def cmul(xr, xi, yr, yi):
        return xr * yr - xi * yi, xr * yi + xi * yr

    bl_re, bl_im = cmul(pw_re[:L, :, :, :, None], pw_im[:L, :, :, :, None], bb_re[None], bb_im[None])
    lag_rows = lambda z: jnp.transpose(z, (1, 2, 0, 4, 3)).reshape(2 * G, L * C, N)
    b_lag = jnp.concatenate([lag_rows(bl_re), -lag_rows(bl_im)], axis=-1)
    c_t = lambda z: jnp.swapaxes(z, -1, -2).reshape(2 * G, N, C)
    c_cat = jnp.concatenate([c_t(c_re), c_t(c_im)], axis=1)
    k_lag = _s5_lag_kernels(b_lag, c_cat).reshape(2, G, L, C, C)

    k_idx = jnp.arange(L)
    lag = k_idx[None, :] - k_idx[:, None]
    kf = jnp.where((lag >= 0)[None, :, :, None, None], k_lag[0][:, jnp.clip(lag, 0, L - 1)], 0.0)
    kr = jnp.where((lag <= 0)[None, :, :, None, None], k_lag[1][:, jnp.clip(-lag, 0, L - 1)], 0.0)
    eye_k = jnp.eye(L, dtype=F32)[None, :, :, None, None]
    d_diag = d_skip.astype(F32).reshape(G, 1, 1, C, 1) * jnp.eye(C, dtype=F32)[None, None, None]
    toep = kf + kr + eye_k * d_diag
    toep = jnp.transpose(toep, (0, 1, 3, 2, 4)).reshape(G, L * C, L * C)

    bf_re, bf_im = cmul(pw_re[L - 1 - k_idx, 0][..., None], pw_im[L - 1 - k_idx, 0][..., None],
                        bb_re[0][None], bb_im[0][None])
    bk_re, bk_im = cmul(pw_re[k_idx, 1][..., None], pw_im[k_idx, 1][..., None],
                        bb_re[1][None], bb_im[1][None])
    to_rows = lambda z: jnp.transpose(z, (1, 0, 3, 2)).reshape(G, L * C, N)
    bc = jnp.concatenate([to_rows(bf_re), to_rows(bk_re), to_rows(bf_im), to_rows(bk_im)], axis=-1)

    pw_g = lambda z, idx, d: jnp.transpose(z[idx, d], (1, 0, 2))[:, :, None, :]
    cf_re, cf_im = cmul(c_re[0][:, None], c_im[0][:, None], pw_g(pw_re, k_idx + 1, 0), pw_g(pw_im, k_idx + 1, 0))
    cb_re, cb_im = cmul(c_re[1][:, None], c_im[1][:, None], pw_g(pw_re, L - k_idx, 1), pw_g(pw_im, L - k_idx, 1))
    to_cols = lambda z: jnp.transpose(z, (0, 3, 1, 2)).reshape(G, N, L * C)
    cc = jnp.concatenate([to_cols(cf_re), to_cols(cb_re), -to_cols(cf_im), -to_cols(cb_im)], axis=1)

    a_chunk = jnp.concatenate([pw_re[L, 0], pw_re[L, 1], pw_im[L, 0], pw_im[L, 1]], axis=-1)
    a_chunk = a_chunk.reshape(S5_SLABS, S5_SLAB_GROUPS, 4 * N)
    return toep.astype(BF16), bc.astype(BF16), cc.astype(BF16), a_chunk


def _s5_mixer(h, ops):
    toep, bc, cc, a_chunk = ops
    m, d = h.shape
    nc = m // S5_CHUNK
    u = jnp.transpose(h.reshape(nc, S5_CHUNK, S5_SLABS, LANES), (2, 0, 1, 3)).reshape(S5_SLABS, nc, S5_SLAB_W)
    v_re, v_im = _s5_state_inputs(u, bc)
    s_re, s_im = _s5_scan(v_re, v_im, a_chunk)
    y = _s5_outputs(u, s_re, s_im, toep, cc)
    return jnp.transpose(y.reshape(S5_SLABS, nc, S5_CHUNK, LANES), (1, 2, 0, 3)).reshape(m, d)


def _na_bias_tables(rpb):
    var = jnp.arange(NA_KH)
    row_rel = jnp.arange(NA_KH)[None, :] - var[:, None] + (NA_KH - 1)
    q = jnp.arange(GRID_W)
    col_start = jnp.clip(q - NA_KW // 2, 0, GRID_W - NA_KW)
    kc = jnp.arange(GRID_W)
    inside = (kc[None, :] >= col_start[:, None]) & (kc[None, :] < col_start[:, None] + NA_KW)
    col_rel = jnp.clip(kc[None, :] - q[:, None] + (NA_KW - 1), 0, 2 * NA_KW - 2)
    t = rpb.astype(F32)[:, row_rel][:, :, :, col_rel]
    t = jnp.where(inside[None, None, None], t, NA_NEG)
    t = jnp.transpose(t, (1, 0, 3, 2, 4)).reshape(NA_KH, rpb.shape[0], GRID_W, NA_KH * GRID_W)

    def step_table(variants, offsets):
        rows = [jnp.pad(t[v], ((0, 0), (0, 0), (o * GRID_W, (NA_BAND - NA_KH - o) * GRID_W)),
                        constant_values=NA_NEG) for v, o in zip(variants, offsets)]
        return jnp.concatenate(rows, axis=1)

    r = NA_ROWS_PER_STEP
    half = NA_KH // 2
    first = step_table(range(r), [0] * r)
    inner = step_table([half] * r, range(r))
    last = step_table(range(half, half + r), [NA_BAND - NA_KH] * r)
    return jnp.stack([first, inner, last])


def _na_body(q_ref, k_ref, v_ref, bias_ref, o_ref, *, rows):
    i = pl.program_id(1)
    scale = NA_HEAD_DIM ** -0.5
    dn = (((1,), (1,)), ((), ()))
    nctx = CTX_LEN // (NA_ROWS_PER_STEP * GRID_W)
    band = NA_BAND * GRID_W
    heads = [pl.ds(hd * NA_HEAD_DIM, NA_HEAD_DIM) for hd in range(NA_HEADS_PER_STEP)]

    def ctx_scores(hd):
        return lax.dot_general(q_ref[:, hd], k_ref[0:CTX_LEN, hd], dn, preferred_element_type=F32) * scale

    @pl.when(i < nctx)
    def _():
        for hd in heads:
            s_c = ctx_scores(hd)
            p = jnp.exp(s_c - jnp.max(s_c, axis=-1, keepdims=True))
            o = jnp.dot(p.astype(BF16), v_ref[0:CTX_LEN, hd], preferred_element_type=F32)
            o_ref[:, hd] = (o / jnp.sum(p, axis=-1, keepdims=True)).astype(o_ref.dtype)

    @pl.when(i >= nctx)
    def _():
        r = (i - nctx) * NA_ROWS_PER_STEP
        rb = jnp.clip(r - NA_KH // 2, 0, rows - NA_BAND)
        kind = jnp.where(r == 0, 0, jnp.where(r == rows - NA_ROWS_PER_STEP, 2, 1))
        start = pl.multiple_of(CTX_LEN + rb * GRID_W, GRID_W)
        for n, hd in enumerate(heads):
            s_c = ctx_scores(hd)
            s_w = lax.dot_general(q_ref[:, hd], k_ref[pl.ds(start, band), hd], dn,
                                  preferred_element_type=F32) * scale + bias_ref[kind, n]
            mx = jnp.maximum(jnp.max(s_w, axis=-1, keepdims=True), jnp.max(s_c, axis=-1, keepdims=True))
            p_w = jnp.exp(s_w - mx)
            p_c = jnp.exp(s_c - mx)
            den = jnp.sum(p_w, axis=-1, keepdims=True) + jnp.sum(p_c, axis=-1, keepdims=True)
            o = (jnp.dot(p_w.astype(BF16), v_ref[pl.ds(start, band), hd], preferred_element_type=F32)
                 + jnp.dot(p_c.astype(BF16), v_ref[0:CTX_LEN, hd], preferred_element_type=F32))
            o_ref[:, hd] = (o / den).astype(o_ref.dtype)


def _na_attention(qkv, bias):
    m = qkv.shape[0]
    rows = (m - CTX_LEN) // GRID_W
    assert rows % NA_ROWS_PER_STEP == 0 and rows >= NA_BAND + NA_ROWS_PER_STEP
    tq = NA_ROWS_PER_STEP * GRID_W
    hs = NA_HEADS_PER_STEP
    w = hs * NA_HEAD_DIM
    nh = NA_HEADS // hs
    return pl.pallas_call(
        functools.partial(_na_body, rows=rows),
        grid=(nh, m // tq),
        in_specs=[pl.BlockSpec((tq, w), lambda h, i: (i, h)),
                  pl.BlockSpec((m, w), lambda h, i: (0, nh + h)),
                  pl.BlockSpec((m, w), lambda h, i: (0, 2 * nh + h)),
                  pl.BlockSpec((3, hs, tq, NA_BAND * GRID_W), lambda h, i: (0, h, 0, 0))],
        out_specs=pl.BlockSpec((tq, w), lambda h, i: (i, h)),
        out_shape=jax.ShapeDtypeStruct((m, D_MODEL), BF16),
        compiler_params=_params("parallel", "arbitrary"),
        name="na_attention",
    )(qkv, qkv, qkv, bias)


def _tiles(m):
    for blocks in range(5, 0, -1):
        if m % (blocks * ROW_BLOCK) == 0:
            return blocks * ROW_BLOCK
    raise ValueError(m)


def kernel(x, c, ctx, c_ctx, ada_w, ada_b, norm_g, s5_a_re, s5_a_im, s5_log_dt, s5_b_re, s5_b_im,
           s5_c_re, s5_c_im, s5_d, s5_w_glu, na_w_qkv, na_w_o, na_rpb, ffn_w_gu, ffn_w_down,
           moe_w_router, moe_w_gu, moe_w_down):
    assert x.shape[0] == 1 and x.shape[2] == D_MODEL and ctx.shape[1] == CTX_LEN
    d = D_MODEL
    xc = jnp.concatenate([ctx[0], x[0]], axis=0).astype(F32)
    m = xc.shape[0]
    tm = _tiles(m)

    cond_t = jnp.stack([c_ctx, c[0]], axis=1).astype(F32)
    mods = _adaln(cond_t, ada_w.astype(F32), ada_b.astype(F32))
    mods = mods.reshape(DEPTH, 2, 6, 1, d)
    mod = lambda i, k: mods[i, :, k]

    h = _prenorm(xc, norm_g[0, 0], mod(0, 0), mod(0, 1))
    for i in range(DEPTH):
        j = i // 2
        last = i == DEPTH - 1
        nxt_f = (norm_g[i, 2], mod(i, 3), mod(i, 4))
        if i % 2 == 0:
            ops = _s5_operators(s5_a_re[j], s5_a_im[j], s5_log_dt[j], s5_b_re[j], s5_b_im[j],
                                s5_c_re[j], s5_c_im[j], s5_d[j])
            g = _s5_mixer(h, ops)
            y = _gated_matmul(g, s5_w_glu[j].astype(BF16)[None], act="glu", tm=tm, tn=512, out_dtype=F32)
            xc, h = _postnorm(y, xc, norm_g[i, 1], mod(i, 2), nxt_f)
            a = _gated_matmul(h, ffn_w_gu[j].astype(BF16)[None], act="swiglu", tm=tm, tn=512,
                              out_dtype=BF16)
            y = _matmul(a, ffn_w_down[j].astype(BF16), tm=tm, tn=512, out_dtype=F32)
        else:
            qkv = _matmul(h, na_w_qkv[j].astype(BF16), tm=tm, tn=1024, out_dtype=BF16)
            o = _na_attention(qkv, _na_bias_tables(na_rpb[j]))
            y = _matmul(o, na_w_o[j].astype(BF16), tm=tm, tn=512, out_dtype=F32)
            xc, h, gates = _postnorm(y, xc, norm_g[i, 1], mod(i, 2), nxt_f, w_router=moe_w_router[j])
            route = jnp.transpose(gates[:, :N_EXPERTS])[:, :, None]
            a = _gated_matmul(h, moe_w_gu[j].astype(BF16), act="swiglu", tm=tm, tn=512,
                              out_dtype=BF16, route=route)
            w_down = moe_w_down[j].astype(BF16).reshape(N_EXPERTS * EXPERT_DIM, d)
            y = _matmul(a, w_down, tm=tm // 2, tn=512, out_dtype=F32)
        if last:
            out, = _postnorm(y, xc, norm_g[i, 3], mod(i, 5), latent_only=True)
            return out[None].astype(x.dtype)
        nxt_m = (norm_g[i + 1, 0], mod(i + 1, 0), mod(i + 1, 1))
        xc, h = _postnorm(y, xc, norm_g[i, 3], mod(i, 5), nxt_m)
```

```python
import functools

import jax
import jax.numpy as jnp
from jax import lax
from jax.experimental import pallas as pl
from jax.experimental.pallas import tpu as pltpu

D_MODEL = 4096
DEPTH = 4
GRID_W = 64
CTX_LEN = 256
NORM_EPS = 1e-6
S5_GROUP = 16
S5_GROUPS = D_MODEL // S5_GROUP
S5_STATE = 64
NA_HEADS = 32
NA_HEAD_DIM = D_MODEL // NA_HEADS
NA_KH = 8
NA_KW = 16
N_EXPERTS = 8
EXPERT_DIM = 1024

LANES = 128
S5_CHUNK = 16
S5_CK = S5_CHUNK * S5_GROUP
S5_SLAB_GROUPS = LANES // S5_GROUP
S5_SLABS = S5_GROUPS // S5_SLAB_GROUPS
S5_SLAB_W = S5_CHUNK * LANES
S5_CHUNK_TILE = 520
NA_ROWS_PER_STEP = 4
NA_HEADS_PER_STEP = 2
NA_BAND = NA_KH + NA_ROWS_PER_STEP
NA_NEG = -1e30
LOG2_E = 1.4426950408889634
NA_Q_SCALE = NA_HEAD_DIM ** -0.5 * LOG2_E
ROW_BLOCK = 256
VMEM_LIMIT_V7X = 56 * 1024 * 1024

BF16 = jnp.bfloat16
F32 = jnp.float32


def _params(*sem):
    return pltpu.CompilerParams(dimension_semantics=sem, vmem_limit_bytes=VMEM_LIMIT_V7X)


def _adaln_body(cond_ref, w_ref, b_ref, o_ref):
    cond = cond_ref[...]
    s = cond * jax.nn.sigmoid(cond)
    w = w_ref[...]
    r0 = jnp.sum(w * s[:, 0:1], axis=0, keepdims=True)
    r1 = jnp.sum(w * s[:, 1:2], axis=0, keepdims=True)
    o_ref[...] = jnp.concatenate([r0, r1], axis=0) + b_ref[...]


def _adaln(cond_t, ada_w, ada_b, tn=512):
    depth, d, n = ada_w.shape
    return pl.pallas_call(
        _adaln_body,
        grid=(depth, n // tn),
        in_specs=[
            pl.BlockSpec((d, 2), lambda l, j: (0, 0)),
            pl.BlockSpec((None, d, tn), lambda l, j: (l, 0, j)),
            pl.BlockSpec((None, 1, tn), lambda l, j: (l, 0, j)),
        ],
        out_specs=pl.BlockSpec((None, 2, tn), lambda l, j: (l, 0, j)),
        out_shape=jax.ShapeDtypeStruct((depth, 2, n), F32),
        compiler_params=_params("parallel", "parallel"),
        name="adaln",
    )(cond_t, ada_w, ada_b.reshape(depth, 1, n))


def _rms(x, g):
    return x * lax.rsqrt(jnp.mean(x * x, axis=-1, keepdims=True) + NORM_EPS) * g


def _route(h, wr_ref, gates_ref):
    logits = jnp.dot(h, wr_ref[...], preferred_element_type=F32, precision=lax.Precision.HIGHEST)
    lane = lax.broadcasted_iota(jnp.int32, logits.shape, 1).astype(F32)
    n_lanes = float(logits.shape[1])
    logits = jnp.where(lane < N_EXPERTS, logits, -jnp.inf)
    m1 = jnp.max(logits, axis=-1, keepdims=True)
    i1 = jnp.min(jnp.where(logits == m1, lane, n_lanes), axis=-1, keepdims=True)
    rest = jnp.where(lane == i1, -jnp.inf, logits)
    m2 = jnp.max(rest, axis=-1, keepdims=True)
    i2 = jnp.min(jnp.where(rest == m2, lane, n_lanes), axis=-1, keepdims=True)
    e2 = jnp.exp(m2 - m1)
    w1 = 1.0 / (1.0 + e2)
    w2 = e2 / (1.0 + e2)
    gates_ref[...] = jnp.where(lane == i1, w1, 0.0) + jnp.where(lane == i2, w2, 0.0)


def _pre_body(x_ref, g_ref, sh_ref, sc_ref, h_ref):
    h = _rms(x_ref[...], g_ref[...]) * (1.0 + sc_ref[...]) + sh_ref[...]
    h_ref[...] = h.astype(h_ref.dtype)


def _post_body(*refs, with_next, with_router):
    y_ref, x_ref, gp_ref, gate_ref = refs[:4]
    x_new = x_ref[...] + gate_ref[...] * _rms(y_ref[...].astype(F32), gp_ref[...])
    if not with_next:
        refs[4][...] = x_new
        return
    gn_ref, sh_ref, sc_ref = refs[4:7]
    rest = refs[7:]
    if with_router:
        wr_ref, rest = rest[0], rest[1:]
    xo_ref, h_ref = rest[:2]
    xo_ref[...] = x_new
    h = _rms(x_new, gn_ref[...]) * (1.0 + sc_ref[...]) + sh_ref[...]
    h_ref[...] = h.astype(h_ref.dtype)
    if with_router:
        _route(h, wr_ref, rest[2])


def _row_spec(d, off=0):
    return pl.BlockSpec((ROW_BLOCK, d), lambda i: (i + off, 0))


def _vec_spec(d):
    return pl.BlockSpec((1, d), lambda i: (0, 0))


def _mod_spec(d, n_ctx_blocks):
    return pl.BlockSpec((None, 1, d), lambda i: (jnp.where(i < n_ctx_blocks, 0, 1), 0, 0))


def _prenorm(x, g, shift, scale):
    m, d = x.shape
    nc = CTX_LEN // ROW_BLOCK
    return pl.pallas_call(
        _pre_body,
        grid=(m // ROW_BLOCK,),
        in_specs=[_row_spec(d), _vec_spec(d), _mod_spec(d, nc), _mod_spec(d, nc)],
        out_specs=_row_spec(d),
        out_shape=jax.ShapeDtypeStruct((m, d), BF16),
        compiler_params=_params("parallel"),
        name="prenorm",
    )(x, g.reshape(1, d), shift, scale)


def _postnorm(y, x, g_post, gate, nxt=None, w_router=None, latent_only=False):
    m, d = x.shape
    nc = CTX_LEN // ROW_BLOCK
    off = nc if latent_only else 0
    n_ctx = 0 if latent_only else nc
    m_out = m - off * ROW_BLOCK
    ins = [y, x, g_post.reshape(1, d), gate]
    in_specs = [_row_spec(d, off), _row_spec(d, off), _vec_spec(d), _mod_spec(d, n_ctx)]
    out_shape = [jax.ShapeDtypeStruct((m_out, d), F32)]
    out_specs = [_row_spec(d)]
    if nxt is not None:
        ins += [nxt[0].reshape(1, d), nxt[1], nxt[2]]
        in_specs += [_vec_spec(d), _mod_spec(d, n_ctx), _mod_spec(d, n_ctx)]
        out_shape.append(jax.ShapeDtypeStruct((m_out, d), BF16))
        out_specs.append(_row_spec(d))
        if w_router is not None:
            wr = jnp.zeros((d, LANES), F32).at[:, :N_EXPERTS].set(w_router)
            ins.append(wr)
            in_specs.append(pl.BlockSpec((d, LANES), lambda i: (0, 0)))
            out_shape.append(jax.ShapeDtypeStruct((m_out, LANES), F32))
            out_specs.append(pl.BlockSpec((ROW_BLOCK, LANES), lambda i: (i, 0)))
    out = pl.pallas_call(
        functools.partial(_post_body, with_next=nxt is not None, with_router=w_router is not None),
        grid=(m_out // ROW_BLOCK,),
        in_specs=in_specs,
        out_specs=out_specs,
        out_shape=out_shape,
        compiler_params=_params("parallel"),
        name="postnorm",
    )(*ins)
    return out


def _row_tile_spec(tm, kk, ngrid):
    return pl.BlockSpec((tm, kk), (lambda i, j: (i, 0)) if ngrid == 2 else (lambda i, x, j: (i, 0)),
                        pipeline_mode=pl.Buffered(1))


def _mm_body(a_ref, w_ref, o_ref, *, scaled_tiles, scale):
    acc = jnp.dot(a_ref[...], w_ref[...], preferred_element_type=F32)
    if scaled_tiles:
        acc = acc * jnp.where(pl.program_id(1) < scaled_tiles, scale, 1.0)
    o_ref[...] = acc.astype(o_ref.dtype)


def _matmul(a, w, layer, *, tm, tn, out_dtype, scaled_cols=0, scale=1.0):
    m, kk = a.shape
    n = w.shape[2]
    assert scaled_cols % tn == 0
    return pl.pallas_call(
        functools.partial(_mm_body, scaled_tiles=scaled_cols // tn, scale=scale),
        grid=(m // tm, n // tn),
        in_specs=[_row_tile_spec(tm, kk, 2),
                  pl.BlockSpec((None, kk, tn), lambda i, j: (layer, 0, j))],
        out_specs=pl.BlockSpec((tm, tn), lambda i, j: (i, j)),
        out_shape=jax.ShapeDtypeStruct((m, n), out_dtype),
        compiler_params=_params("parallel", "parallel"),
        name="matmul",
    )(a, w)


def _gated_body(*refs, act, with_route):
    a_ref, wg_ref, wu_ref = refs[:3]
    o_ref = refs[-1]
    a = a_ref[...]
    g = jnp.dot(a, wg_ref[...], preferred_element_type=F32)
    u = jnp.dot(a, wu_ref[...], preferred_element_type=F32)
    if act == "swiglu":
        r = g * jax.nn.sigmoid(g) * u
    else:
        r = g * jax.nn.sigmoid(u)
    if with_route:
        gates = refs[3][...]
        lane = lax.broadcasted_iota(jnp.int32, gates.shape, 1)
        r = r * jnp.sum(jnp.where(lane == pl.program_id(1), gates, 0.0), axis=1, keepdims=True)
    o_ref[...] = r.astype(o_ref.dtype)


def _gated_matmul(a, w, first, count, *, act, tm, tn, out_dtype, gates=None):
    m, kk = a.shape
    f2 = w.shape[2]
    nf = f2 // 2 // tn
    ins = [a, w, w]
    in_specs = [_row_tile_spec(tm, kk, 3),
                pl.BlockSpec((None, kk, tn), lambda i, x, j: (first + x, 0, j)),
                pl.BlockSpec((None, kk, tn), lambda i, x, j: (first + x, 0, nf + j))]
    if gates is not None:
        ins.append(gates)
        in_specs.append(pl.BlockSpec((tm, LANES), lambda i, x, j: (i, 0)))
    return pl.pallas_call(
        functools.partial(_gated_body, act=act, with_route=gates is not None),
        grid=(m // tm, count, nf),
        in_specs=in_specs,
        out_specs=pl.BlockSpec((tm, tn), lambda i, x, j: (i, x * nf + j)),
        out_shape=jax.ShapeDtypeStruct((m, count * f2 // 2), out_dtype),
        compiler_params=_params("parallel", "parallel", "parallel"),
        name="gated_matmul",
    )(*ins)


def _s5_lag_body(b_ref, c_ref, o_ref):
    for g in range(b_ref.shape[0]):
        o_ref[g] = jnp.dot(b_ref[g], c_ref[g], preferred_element_type=F32,
                           precision=lax.Precision.HIGHEST)


def _s5_lag_kernels(b_lag, c_cat):
    g, r, n2 = b_lag.shape
    gb = S5_SLAB_GROUPS
    return pl.pallas_call(
        _s5_lag_body,
        grid=(g // gb,),
        in_specs=[pl.BlockSpec((gb, r, n2), lambda i: (i, 0, 0)),
                  pl.BlockSpec((gb, n2, S5_GROUP), lambda i: (i, 0, 0))],
        out_specs=pl.BlockSpec((gb, r, S5_GROUP), lambda i: (i, 0, 0)),
        out_shape=jax.ShapeDtypeStruct((g, r, S5_GROUP), F32),
        compiler_params=_params("parallel"),
        name="s5_lag_kernels",
    )(b_lag, c_cat)


def _chunk_tile(nc):
    return S5_CHUNK_TILE if nc % S5_CHUNK_TILE == 0 else nc


def _slab_rows(g, k):
    return pl.ds(k * LANES + g * S5_GROUP, S5_GROUP)


def _s5_in_body(u_ref, bc_ref, vre_ref, vim_ref, w_ref, *, tc):
    first = jnp.logical_and(pl.program_id(0) == 0, pl.program_id(1) == 0)

    @pl.when(first)
    def _():
        w_ref[...] = jnp.zeros_like(w_ref)

    @pl.when(pl.program_id(1) == 0)
    def _():
        for g in range(S5_SLAB_GROUPS):
            for k in range(S5_CHUNK):
                w_ref[g, _slab_rows(g, k), :] = bc_ref[g, k * S5_GROUP:(k + 1) * S5_GROUP, :]

    u = u_ref[...]
    for g in range(S5_SLAB_GROUPS):
        v = jnp.dot(u, w_ref[g], preferred_element_type=F32)
        rows = pl.ds(g, tc, stride=S5_SLAB_GROUPS)
        vre_ref[rows, :] = v[:, :LANES]
        vim_ref[rows, :] = v[:, LANES:]


def _s5_state_inputs(u, bc):
    ns, nc, sw = u.shape
    tc = _chunk_tile(nc)
    gb = S5_SLAB_GROUPS
    n4 = 4 * S5_STATE
    out = pl.BlockSpec((None, tc * gb, LANES), lambda s, m: (s, m, 0))
    return pl.pallas_call(
        functools.partial(_s5_in_body, tc=tc),
        grid=(ns, nc // tc),
        in_specs=[pl.BlockSpec((None, tc, sw), lambda s, m: (s, m, 0)),
                  pl.BlockSpec((gb, S5_CK, n4), lambda s, m: (s, 0, 0))],
        out_specs=[out, out],
        out_shape=[jax.ShapeDtypeStruct((ns, nc * gb, LANES), F32)] * 2,
        scratch_shapes=[pltpu.VMEM((gb, sw, n4), BF16)],
        compiler_params=_params("arbitrary", "arbitrary"),
        name="s5_state_inputs",
    )(u, bc)


def _s5_scan_body(vre_ref, vim_ref, a_ref, sre_ref, sim_ref, *, nc, nctx):
    gb = S5_SLAB_GROUPS
    a_re, a_im = a_ref[:, :LANES], a_ref[:, LANES:]
    fwd = lax.broadcasted_iota(jnp.int32, (gb, LANES), 1) < S5_STATE
    bwd = jnp.logical_not(fwd)

    def step(t, carry):
        s_re, s_im = carry
        t_b = jnp.where(t < nctx, nctx - 1 - t, nc - 1 - (t - nctx))
        rf = pl.ds(pl.multiple_of(t * gb, gb), gb)
        rb = pl.ds(pl.multiple_of(t_b * gb, gb), gb)
        pltpu.store(sre_ref.at[rf, :], s_re, mask=fwd)
        pltpu.store(sre_ref.at[rb, :], s_re, mask=bwd)
        pltpu.store(sim_ref.at[rf, :], s_im, mask=fwd)
        pltpu.store(sim_ref.at[rb, :], s_im, mask=bwd)
        v_re = jnp.where(fwd, vre_ref[rf, :], vre_ref[rb, :])
        v_im = jnp.where(fwd, vim_ref[rf, :], vim_ref[rb, :])
        return a_re * s_re - a_im * s_im + v_re, a_re * s_im + a_im * s_re + v_im

    zero = jnp.zeros((gb, LANES), F32)
    lax.fori_loop(0, nc, step, (zero, zero))


def _s5_scan(v_re, v_im, a_chunk):
    ns, rows, _ = v_re.shape
    gb = S5_SLAB_GROUPS
    nc = rows // gb
    blk = pl.BlockSpec((None, rows, LANES), lambda s: (s, 0, 0))
    return pl.pallas_call(
        functools.partial(_s5_scan_body, nc=nc, nctx=CTX_LEN // S5_CHUNK),
        grid=(ns,),
        in_specs=[blk, blk, pl.BlockSpec((None, gb, 2 * LANES), lambda s: (s, 0, 0))],
        out_specs=[blk, blk],
        out_shape=[jax.ShapeDtypeStruct((ns, rows, LANES), F32)] * 2,
        compiler_params=_params("parallel"),
        name="s5_scan",
    )(v_re, v_im, a_chunk)


def _s5_out_body(u_ref, sre_ref, sim_ref, t_ref, c_ref, y_ref, wt_ref, wc_ref, scat_ref, *, tc):
    gb = S5_SLAB_GROUPS
    n4 = 4 * S5_STATE

    @pl.when(pl.program_id(1) == 0)
    def _():
        row = lax.broadcasted_iota(jnp.int32, (S5_CK, S5_SLAB_W), 0)
        col = lax.broadcasted_iota(jnp.int32, (S5_CK, S5_SLAB_W), 1)
        spread = (row // S5_GROUP) * LANES + row % S5_GROUP
        for g in range(gb):
            place = (col == spread + g * S5_GROUP).astype(BF16)
            tg = jnp.dot(t_ref[g], place, preferred_element_type=F32).astype(BF16)
            for k in range(S5_CHUNK):
                wt_ref[_slab_rows(g, k), :] = tg[k * S5_GROUP:(k + 1) * S5_GROUP, :]
            wc_ref[g * n4:(g + 1) * n4, :] = jnp.dot(c_ref[g], place, preferred_element_type=F32).astype(BF16)

    for g in range(gb):
        rows = pl.ds(g, tc, stride=gb)
        scat_ref[:, g * n4:g * n4 + LANES] = sre_ref[rows, :].astype(BF16)
        scat_ref[:, g * n4 + LANES:(g + 1) * n4] = sim_ref[rows, :].astype(BF16)
    y = (jnp.dot(u_ref[...], wt_ref[...], preferred_element_type=F32)
         + jnp.dot(scat_ref[...], wc_ref[...], preferred_element_type=F32))
    y_ref[...] = jax.nn.gelu(y).astype(y_ref.dtype)


def _s5_outputs(u, s_re, s_im, toep, cc):
    ns, nc, sw = u.shape
    tc = _chunk_tile(nc)
    gb = S5_SLAB_GROUPS
    n4 = 4 * S5_STATE
    state = pl.BlockSpec((None, tc * gb, LANES), lambda i, m: (i, m, 0))
    return pl.pallas_call(
        functools.partial(_s5_out_body, tc=tc),
        grid=(ns, nc // tc),
        in_specs=[pl.BlockSpec((None, tc, sw), lambda i, m: (i, m, 0)), state, state,
                  pl.BlockSpec((gb, S5_CK, S5_CK), lambda i, m: (i, 0, 0)),
                  pl.BlockSpec((gb, n4, S5_CK), lambda i, m: (i, 0, 0))],
        out_specs=pl.BlockSpec((None, tc, sw), lambda i, m: (i, m, 0)),
        out_shape=jax.ShapeDtypeStruct((ns, nc, sw), BF16),
        scratch_shapes=[pltpu.VMEM((sw, sw), BF16), pltpu.VMEM((gb * n4, sw), BF16),
                        pltpu.VMEM((tc, gb * n4), BF16)],
        compiler_params=_params("parallel", "arbitrary"),
        name="s5_outputs",
    )(u, s_re, s_im, toep, cc)


def _s5_operators(a_re, a_im, log_dt, b_re, b_im, c_re, c_im, d_skip):
    L, G, N, C = S5_CHUNK, S5_GROUPS, S5_STATE, S5_GROUP
    a_re, a_im = a_re.astype(F32), a_im.astype(F32)
    b_re, b_im = b_re.astype(F32), b_im.astype(F32)
    c_re, c_im = c_re.astype(F32), c_im.astype(F32)
    dt = jnp.exp(log_dt.astype(F32))[..., None]
    steps = jnp.arange(L + 1, dtype=F32)[:, None, None, None]
    mag = jnp.exp((a_re * dt)[None] * steps)
    ang = (a_im * dt)[None] * steps
    pw_re, pw_im = mag * jnp.cos(ang), mag * jnp.sin(ang)
    p, q = pw_re[1] - 1.0, pw_im[1]
    den = a_re * a_re + a_im * a_im
    f_re, f_im = (p * a_re + q * a_im) / den, (q * a_re - p * a_im) / den
    bb_re = f_re[..., None] * b_re - f_im[..., None] * b_im
    bb_im = f_re[..., None] * b_im + f_im[..., None] * b_re

    def cmul(xr, xi, yr, yi):
        return xr * yr - xi * yi, xr * yi + xi * yr

    bl_re, bl_im = cmul(pw_re[:L, :, :, :, None], pw_im[:L, :, :, :, None], bb_re[None], bb_im[None])
    lag_rows = lambda z: jnp.transpose(z, (1, 2, 0, 4, 3)).reshape(2 * G, L * C, N)
    b_lag = jnp.concatenate([lag_rows(bl_re), -lag_rows(bl_im)], axis=-1)
    c_t = lambda z: jnp.swapaxes(z, -1, -2).reshape(2 * G, N, C)
    c_cat = jnp.concatenate([c_t(c_re), c_t(c_im)], axis=1)
    k_lag = _s5_lag_kernels(b_lag, c_cat).reshape(2, G, L, C, C)

    k_idx = jnp.arange(L)
    lag = k_idx[None, :] - k_idx[:, None]
    kf = jnp.where((lag >= 0)[None, :, :, None, None], k_lag[0][:, jnp.clip(lag, 0, L - 1)], 0.0)
    kr = jnp.where((lag <= 0)[None, :, :, None, None], k_lag[1][:, jnp.clip(-lag, 0, L - 1)], 0.0)
    eye_k = jnp.eye(L, dtype=F32)[None, :, :, None, None]
    d_diag = d_skip.astype(F32).reshape(G, 1, 1, C, 1) * jnp.eye(C, dtype=F32)[None, None, None]
    toep = kf + kr + eye_k * d_diag
    toep = jnp.transpose(toep, (0, 1, 3, 2, 4)).reshape(G, L * C, L * C)

    bf_re, bf_im = cmul(pw_re[L - 1 - k_idx, 0][..., None], pw_im[L - 1 - k_idx, 0][..., None],
                        bb_re[0][None], bb_im[0][None])
    bk_re, bk_im = cmul(pw_re[k_idx, 1][..., None], pw_im[k_idx, 1][..., None],
                        bb_re[1][None], bb_im[1][None])
    to_rows = lambda z: jnp.transpose(z, (1, 0, 3, 2)).reshape(G, L * C, N)
    bc = jnp.concatenate([to_rows(bf_re), to_rows(bk_re), to_rows(bf_im), to_rows(bk_im)], axis=-1)

    pw_g = lambda z, idx, d: jnp.transpose(z[idx, d], (1, 0, 2))[:, :, None, :]
    cf_re, cf_im = cmul(c_re[0][:, None], c_im[0][:, None], pw_g(pw_re, k_idx + 1, 0), pw_g(pw_im, k_idx + 1, 0))
    cb_re, cb_im = cmul(c_re[1][:, None], c_im[1][:, None], pw_g(pw_re, L - k_idx, 1), pw_g(pw_im, L - k_idx, 1))
    to_cols = lambda z: jnp.transpose(z, (0, 3, 1, 2)).reshape(G, N, L * C)
    cc = jnp.concatenate([to_cols(cf_re), to_cols(cb_re), -to_cols(cf_im), -to_cols(cb_im)], axis=1)

    a_chunk = jnp.concatenate([pw_re[L, 0], pw_re[L, 1], pw_im[L, 0], pw_im[L, 1]], axis=-1)
    a_chunk = a_chunk.reshape(S5_SLABS, S5_SLAB_GROUPS, 4 * N)
    return toep.astype(BF16), bc.astype(BF16), cc.astype(BF16), a_chunk


def _s5_mixer(h, ops):
    toep, bc, cc, a_chunk = ops
    m, d = h.shape
    nc = m // S5_CHUNK
    u = jnp.transpose(h.reshape(nc, S5_CHUNK, S5_SLABS, LANES), (2, 0, 1, 3)).reshape(S5_SLABS, nc, S5_SLAB_W)
    v_re, v_im = _s5_state_inputs(u, bc)
    s_re, s_im = _s5_scan(v_re, v_im, a_chunk)
    y = _s5_outputs(u, s_re, s_im, toep, cc)
    return jnp.transpose(y.reshape(S5_SLABS, nc, S5_CHUNK, LANES), (1, 2, 0, 3)).reshape(m, d)


def _na_bias_tables(rpb):
    var = jnp.arange(NA_KH)
    row_rel = jnp.arange(NA_KH)[None, :] - var[:, None] + (NA_KH - 1)
    q = jnp.arange(GRID_W)
    col_start = jnp.clip(q - NA_KW // 2, 0, GRID_W - NA_KW)
    kc = jnp.arange(GRID_W)
    inside = (kc[None, :] >= col_start[:, None]) & (kc[None, :] < col_start[:, None] + NA_KW)
    col_rel = jnp.clip(kc[None, :] - q[:, None] + (NA_KW - 1), 0, 2 * NA_KW - 2)
    t = rpb.astype(F32)[:, row_rel][:, :, :, col_rel]
    t = jnp.where(inside[None, None, None], t, NA_NEG)
    t = jnp.transpose(t, (1, 0, 3, 2, 4)).reshape(NA_KH, rpb.shape[0], GRID_W, NA_KH * GRID_W)

    def step_table(variants, offsets):
        rows = [jnp.pad(t[v], ((0, 0), (0, 0), (o * GRID_W, (NA_BAND - NA_KH - o) * GRID_W)),
                        constant_values=NA_NEG) for v, o in zip(variants, offsets)]
        return jnp.concatenate(rows, axis=1)

    r = NA_ROWS_PER_STEP
    half = NA_KH // 2
    first = step_table(range(r), [0] * r)
    inner = step_table([half] * r, range(r))
    last = step_table(range(half, half + r), [NA_BAND - NA_KH] * r)
    return jnp.stack([first, inner, last]) * LOG2_E


def _na_body(q_ref, k_ref, v_ref, bias_ref, o_ref, *, rows):
    i = pl.program_id(1)
    dn = (((1,), (1,)), ((), ()))
    nctx = CTX_LEN // (NA_ROWS_PER_STEP * GRID_W)
    band = NA_BAND * GRID_W
    heads = [pl.ds(hd * NA_HEAD_DIM, NA_HEAD_DIM) for hd in range(NA_HEADS_PER_STEP)]

    def ctx_scores(hd):
        return lax.dot_general(q_ref[:, hd], k_ref[0:CTX_LEN, hd], dn, preferred_element_type=F32)

    @pl.when(i < nctx)
    def _():
        for hd in heads:
            s_c = ctx_scores(hd)
            p = jnp.exp2(s_c - jnp.max(s_c, axis=-1, keepdims=True))
            o = jnp.dot(p.astype(BF16), v_ref[0:CTX_LEN, hd], preferred_element_type=F32)
            o_ref[:, hd] = (o / jnp.sum(p, axis=-1, keepdims=True)).astype(o_ref.dtype)

    @pl.when(i >= nctx)
    def _():
        r = (i - nctx) * NA_ROWS_PER_STEP
        rb = jnp.clip(r - NA_KH // 2, 0, rows - NA_BAND)
        kind = jnp.where(r == 0, 0, jnp.where(r == rows - NA_ROWS_PER_STEP, 2, 1))
        start = pl.multiple_of(CTX_LEN + rb * GRID_W, GRID_W)
        for n, hd in enumerate(heads):
            s_c = ctx_scores(hd)
            s_w = lax.dot_general(q_ref[:, hd], k_ref[pl.ds(start, band), hd], dn,
                                  preferred_element_type=F32) + bias_ref[kind, n]
            mx = jnp.maximum(jnp.max(s_w, axis=-1, keepdims=True), jnp.max(s_c, axis=-1, keepdims=True))
            p_w = jnp.exp2(s_w - mx)
            p_c = jnp.exp2(s_c - mx)
            den = jnp.sum(p_w, axis=-1, keepdims=True) + jnp.sum(p_c, axis=-1, keepdims=True)
            o = (jnp.dot(p_w.astype(BF16), v_ref[pl.ds(start, band), hd], preferred_element_type=F32)
                 + jnp.dot(p_c.astype(BF16), v_ref[0:CTX_LEN, hd], preferred_element_type=F32))
            o_ref[:, hd] = (o / den).astype(o_ref.dtype)


def _na_attention(qkv, bias):
    m = qkv.shape[0]
    rows = (m - CTX_LEN) // GRID_W
    assert rows % NA_ROWS_PER_STEP == 0 and rows >= NA_BAND + NA_ROWS_PER_STEP
    tq = NA_ROWS_PER_STEP * GRID_W
    hs = NA_HEADS_PER_STEP
    w = hs * NA_HEAD_DIM
    nh = NA_HEADS // hs
    return pl.pallas_call(
        functools.partial(_na_body, rows=rows),
        grid=(nh, m // tq),
        in_specs=[pl.BlockSpec((tq, w), lambda h, i: (i, h)),
                  pl.BlockSpec((m, w), lambda h, i: (0, nh + h)),
                  pl.BlockSpec((m, w), lambda h, i: (0, 2 * nh + h)),
                  pl.BlockSpec((3, hs, tq, NA_BAND * GRID_W), lambda h, i: (0, h, 0, 0))],
        out_specs=pl.BlockSpec((tq, w), lambda h, i: (i, h)),
        out_shape=jax.ShapeDtypeStruct((m, D_MODEL), BF16),
        compiler_params=_params("parallel", "arbitrary"),
        name="na_attention",
    )(qkv, qkv, qkv, bias)


def _tiles(m):
    for blocks in range(5, 0, -1):
        if m % (blocks * ROW_BLOCK) == 0:
            return blocks * ROW_BLOCK
    raise ValueError(m)


def kernel(x, c, ctx, c_ctx, ada_w, ada_b, norm_g, s5_a_re, s5_a_im, s5_log_dt, s5_b_re, s5_b_im,
           s5_c_re, s5_c_im, s5_d, s5_w_glu, na_w_qkv, na_w_o, na_rpb, ffn_w_gu, ffn_w_down,
           moe_w_router, moe_w_gu, moe_w_down):
    assert x.shape[0] == 1 and x.shape[2] == D_MODEL and ctx.shape[1] == CTX_LEN
    d = D_MODEL
    xc = jnp.concatenate([ctx[0], x[0]], axis=0).astype(F32)
    m = xc.shape[0]
    tm = _tiles(m)

    cond_t = jnp.stack([c_ctx, c[0]], axis=1).astype(F32)
    mods = _adaln(cond_t, ada_w.astype(F32), ada_b.astype(F32))
    mods = mods.reshape(DEPTH, 2, 6, 1, d)
    mod = lambda i, k: mods[i, :, k]

    w_glu, w_ffn_gu, w_ffn_down = s5_w_glu.astype(BF16), ffn_w_gu.astype(BF16), ffn_w_down.astype(BF16)
    w_qkv, w_o = na_w_qkv.astype(BF16), na_w_o.astype(BF16)
    n_odd = moe_w_gu.shape[0]
    w_moe_gu = moe_w_gu.astype(BF16).reshape(n_odd * N_EXPERTS, d, 2 * EXPERT_DIM)
    w_moe_down = moe_w_down.astype(BF16).reshape(n_odd, N_EXPERTS * EXPERT_DIM, d)

    h = _prenorm(xc, norm_g[0, 0], mod(0, 0), mod(0, 1))
    for i in range(DEPTH):
        j = i // 2
        last = i == DEPTH - 1
        nxt_f = (norm_g[i, 2], mod(i, 3), mod(i, 4))
        if i % 2 == 0:
            ops = _s5_operators(s5_a_re[j], s5_a_im[j], s5_log_dt[j], s5_b_re[j], s5_b_im[j],
                                s5_c_re[j], s5_c_im[j], s5_d[j])
            g = _s5_mixer(h, ops)
            y = _gated_matmul(g, w_glu, j, 1, act="glu", tm=tm, tn=512, out_dtype=BF16)
            xc, h = _postnorm(y, xc, norm_g[i, 1], mod(i, 2), nxt_f)
            a = _gated_matmul(h, w_ffn_gu, j, 1, act="swiglu", tm=tm, tn=512, out_dtype=BF16)
            y = _matmul(a, w_ffn_down, j, tm=tm, tn=1024, out_dtype=BF16)
        else:
            qkv = _matmul(h, w_qkv, j, tm=tm, tn=1024, out_dtype=BF16, scaled_cols=d, scale=NA_Q_SCALE)
            o = _na_attention(qkv, _na_bias_tables(na_rpb[j]))
            y = _matmul(o, w_o, j, tm=tm, tn=1024, out_dtype=BF16)
            xc, h, gates = _postnorm(y, xc, norm_g[i, 1], mod(i, 2), nxt_f, w_router=moe_w_router[j])
            a = _gated_matmul(h, w_moe_gu, j * N_EXPERTS, N_EXPERTS, act="swiglu", tm=tm, tn=512,
                              out_dtype=BF16, gates=gates)
            y = _matmul(a, w_moe_down, j, tm=tm, tn=512, out_dtype=BF16)
        if last:
            out, = _postnorm(y, xc, norm_g[i, 3], mod(i, 5), latent_only=True)
            return out[None].astype(x.dtype)
        nxt_m = (norm_g[i + 1, 0], mod(i + 1, 0), mod(i + 1, 1))
        xc, h = _postnorm(y, xc, norm_g[i, 3], mod(i, 5), nxt_m)
```

```python
import functools

import jax
import jax.numpy as jnp
from jax import lax
from jax.experimental import pallas as pl
from jax.experimental.pallas import tpu as pltpu

D_MODEL = 4096
DEPTH = 4
GRID_W = 64
CTX_LEN = 256
NORM_EPS = 1e-6
S5_GROUP = 16
S5_GROUPS = D_MODEL // S5_GROUP
S5_STATE = 64
NA_HEADS = 32
NA_HEAD_DIM = D_MODEL // NA_HEADS
NA_KH = 8
NA_KW = 16
N_EXPERTS = 8
EXPERT_DIM = 1024

LANES = 128
S5_CHUNK = 16
S5_CK = S5_CHUNK * S5_GROUP
S5_SLAB_GROUPS = LANES // S5_GROUP
S5_SLABS = S5_GROUPS // S5_SLAB_GROUPS
S5_SLAB_W = S5_CHUNK * LANES
S5_CHUNK_TILE = 520
NA_ROWS_PER_STEP = 4
NA_HEADS_PER_STEP = 2
NA_BAND = NA_KH + NA_ROWS_PER_STEP
NA_NEG = -1e30
LOG2_E = 1.4426950408889634
NA_Q_SCALE = NA_HEAD_DIM ** -0.5 * LOG2_E
ROW_BLOCK = 256
VMEM_LIMIT_V7X = 56 * 1024 * 1024

BF16 = jnp.bfloat16
F32 = jnp.float32


def _params(*sem):
    return pltpu.CompilerParams(dimension_semantics=sem, vmem_limit_bytes=VMEM_LIMIT_V7X)


def _adaln_body(cond_ref, w_ref, b_ref, o_ref):
    cond = cond_ref[...]
    s = cond * jax.nn.sigmoid(cond)
    w = w_ref[...]
    r0 = jnp.sum(w * s[:, 0:1], axis=0, keepdims=True)
    r1 = jnp.sum(w * s[:, 1:2], axis=0, keepdims=True)
    o_ref[...] = jnp.concatenate([r0, r1], axis=0) + b_ref[...]


def _adaln(cond_t, ada_w, ada_b, tn=512):
    depth, d, n = ada_w.shape
    return pl.pallas_call(
        _adaln_body,
        grid=(depth, n // tn),
        in_specs=[
            pl.BlockSpec((d, 2), lambda l, j: (0, 0)),
            pl.BlockSpec((None, d, tn), lambda l, j: (l, 0, j)),
            pl.BlockSpec((None, 1, tn), lambda l, j: (l, 0, j)),
        ],
        out_specs=pl.BlockSpec((None, 2, tn), lambda l, j: (l, 0, j)),
        out_shape=jax.ShapeDtypeStruct((depth, 2, n), F32),
        compiler_params=_params("parallel", "parallel"),
        name="adaln",
    )(cond_t, ada_w, ada_b.reshape(depth, 1, n))


def _rms(x, g):
    return x * lax.rsqrt(jnp.mean(x * x, axis=-1, keepdims=True) + NORM_EPS) * g


def _route(h, wr_ref, gates_ref):
    logits = jnp.dot(h, wr_ref[...], preferred_element_type=F32, precision=lax.Precision.HIGHEST)
    lane = lax.broadcasted_iota(jnp.int32, logits.shape, 1).astype(F32)
    n_lanes = float(logits.shape[1])
    logits = jnp.where(lane < N_EXPERTS, logits, -jnp.inf)
    m1 = jnp.max(logits, axis=-1, keepdims=True)
    i1 = jnp.min(jnp.where(logits == m1, lane, n_lanes), axis=-1, keepdims=True)
    rest = jnp.where(lane == i1, -jnp.inf, logits)
    m2 = jnp.max(rest, axis=-1, keepdims=True)
    i2 = jnp.min(jnp.where(rest == m2, lane, n_lanes), axis=-1, keepdims=True)
    e2 = jnp.exp(m2 - m1)
    w1 = 1.0 / (1.0 + e2)
    w2 = e2 / (1.0 + e2)
    gates_ref[...] = jnp.where(lane == i1, w1, 0.0) + jnp.where(lane == i2, w2, 0.0)


def _pre_body(x_ref, g_ref, sh_ref, sc_ref, h_ref):
    h = _rms(x_ref[...], g_ref[...]) * (1.0 + sc_ref[...]) + sh_ref[...]
    h_ref[...] = h.astype(h_ref.dtype)


def _post_body(*refs, with_next, with_router):
    y_ref, x_ref, gp_ref, gate_ref = refs[:4]
    x_new = x_ref[...] + gate_ref[...] * _rms(y_ref[...].astype(F32), gp_ref[...])
    if not with_next:
        refs[4][...] = x_new
        return
    gn_ref, sh_ref, sc_ref = refs[4:7]
    rest = refs[7:]
    if with_router:
        wr_ref, rest = rest[0], rest[1:]
    xo_ref, h_ref = rest[:2]
    xo_ref[...] = x_new
    h = _rms(x_new, gn_ref[...]) * (1.0 + sc_ref[...]) + sh_ref[...]
    h_ref[...] = h.astype(h_ref.dtype)
    if with_router:
        _route(h, wr_ref, rest[2])


def _row_spec(d, off=0):
    return pl.BlockSpec((ROW_BLOCK, d), lambda i: (i + off, 0))


def _vec_spec(d):
    return pl.BlockSpec((1, d), lambda i: (0, 0))


def _mod_spec(d, n_ctx_blocks):
    return pl.BlockSpec((None, 1, d), lambda i: (jnp.where(i < n_ctx_blocks, 0, 1), 0, 0))


def _prenorm(x, g, shift, scale):
    m, d = x.shape
    nc = CTX_LEN // ROW_BLOCK
    return pl.pallas_call(
        _pre_body,
        grid=(m // ROW_BLOCK,),
        in_specs=[_row_spec(d), _vec_spec(d), _mod_spec(d, nc), _mod_spec(d, nc)],
        out_specs=_row_spec(d),
        out_shape=jax.ShapeDtypeStruct((m, d), BF16),
        compiler_params=_params("parallel"),
        name="prenorm",
    )(x, g.reshape(1, d), shift, scale)


def _postnorm(y, x, g_post, gate, nxt=None, w_router=None, latent_only=False, h_dtype=BF16):
    m, d = x.shape
    nc = CTX_LEN // ROW_BLOCK
    off = nc if latent_only else 0
    n_ctx = 0 if latent_only else nc
    m_out = m - off * ROW_BLOCK
    ins = [y, x, g_post.reshape(1, d), gate]
    in_specs = [_row_spec(d, off), _row_spec(d, off), _vec_spec(d), _mod_spec(d, n_ctx)]
    out_shape = [jax.ShapeDtypeStruct((m_out, d), F32)]
    out_specs = [_row_spec(d)]
    if nxt is not None:
        ins += [nxt[0].reshape(1, d), nxt[1], nxt[2]]
        in_specs += [_vec_spec(d), _mod_spec(d, n_ctx), _mod_spec(d, n_ctx)]
        out_shape.append(jax.ShapeDtypeStruct((m_out, d), h_dtype))
        out_specs.append(_row_spec(d))
        if w_router is not None:
            wr = jnp.zeros((d, LANES), F32).at[:, :N_EXPERTS].set(w_router)
            ins.append(wr)
            in_specs.append(pl.BlockSpec((d, LANES), lambda i: (0, 0)))
            out_shape.append(jax.ShapeDtypeStruct((m_out, LANES), F32))
            out_specs.append(pl.BlockSpec((ROW_BLOCK, LANES), lambda i: (i, 0)))
    out = pl.pallas_call(
        functools.partial(_post_body, with_next=nxt is not None, with_router=w_router is not None),
        grid=(m_out // ROW_BLOCK,),
        in_specs=in_specs,
        out_specs=out_specs,
        out_shape=out_shape,
        compiler_params=_params("parallel"),
        name="postnorm",
    )(*ins)
    return out


def _row_tile_spec(tm, kk, ngrid):
    return pl.BlockSpec((tm, kk), (lambda i, j: (i, 0)) if ngrid == 2 else (lambda i, x, j: (i, 0)),
                        pipeline_mode=pl.Buffered(1))


def _mm_body(a_ref, w_ref, o_ref, *, scaled_tiles, scale):
    acc = jnp.dot(a_ref[...], w_ref[...], preferred_element_type=F32)
    if scaled_tiles:
        acc = acc * jnp.where(pl.program_id(1) < scaled_tiles, scale, 1.0)
    o_ref[...] = acc.astype(o_ref.dtype)


def _matmul(a, w, layer, *, tm, tn, out_dtype, scaled_cols=0, scale=1.0):
    m, kk = a.shape
    n = w.shape[2]
    assert scaled_cols % tn == 0
    return pl.pallas_call(
        functools.partial(_mm_body, scaled_tiles=scaled_cols // tn, scale=scale),
        grid=(m // tm, n // tn),
        in_specs=[_row_tile_spec(tm, kk, 2),
                  pl.BlockSpec((None, kk, tn), lambda i, j: (layer, 0, j))],
        out_specs=pl.BlockSpec((tm, tn), lambda i, j: (i, j)),
        out_shape=jax.ShapeDtypeStruct((m, n), out_dtype),
        compiler_params=_params("parallel", "parallel"),
        name="matmul",
    )(a, w)


def _gated_body(*refs, act, with_route):
    a_ref, wg_ref, wu_ref = refs[:3]
    o_ref = refs[-1]
    a = a_ref[...]
    g = jnp.dot(a, wg_ref[...], preferred_element_type=F32)
    u = jnp.dot(a, wu_ref[...], preferred_element_type=F32)
    if act == "swiglu":
        r = g * jax.nn.sigmoid(g) * u
    else:
        r = g * jax.nn.sigmoid(u)
    if with_route:
        gates = refs[3][...]
        lane = lax.broadcasted_iota(jnp.int32, gates.shape, 1)
        r = r * jnp.sum(jnp.where(lane == pl.program_id(1), gates, 0.0), axis=1, keepdims=True)
    o_ref[...] = r.astype(o_ref.dtype)


def _gated_matmul(a, w, first, count, *, act, tm, tn, out_dtype, gates=None):
    m, kk = a.shape
    f2 = w.shape[2]
    nf = f2 // 2 // tn
    ins = [a, w, w]
    in_specs = [_row_tile_spec(tm, kk, 3),
                pl.BlockSpec((None, kk, tn), lambda i, x, j: (first + x, 0, j)),
                pl.BlockSpec((None, kk, tn), lambda i, x, j: (first + x, 0, nf + j))]
    if gates is not None:
        ins.append(gates)
        in_specs.append(pl.BlockSpec((tm, LANES), lambda i, x, j: (i, 0)))
    return pl.pallas_call(
        functools.partial(_gated_body, act=act, with_route=gates is not None),
        grid=(m // tm, count, nf),
        in_specs=in_specs,
        out_specs=pl.BlockSpec((tm, tn), lambda i, x, j: (i, x * nf + j)),
        out_shape=jax.ShapeDtypeStruct((m, count * f2 // 2), out_dtype),
        compiler_params=_params("parallel", "parallel", "parallel"),
        name="gated_matmul",
    )(*ins)


def _moe_routing(gates, tm):
    m = gates.shape[0]
    g8 = gates[:, :N_EXPERTS]
    sel = g8 > 0.0
    pos = jnp.cumsum(sel.astype(jnp.int32), axis=0) - 1
    tiles_e = (pos[-1] + tm) // tm
    tile_end = jnp.cumsum(tiles_e)
    slot = ((tile_end - tiles_e) * tm)[None, :] + pos
    n_tiles = (2 * m + tm - 1) // tm + N_EXPERTS
    n_slots = n_tiles * tm
    token = jnp.broadcast_to(jnp.arange(m, dtype=jnp.int32)[:, None], slot.shape)
    token_of_slot = jnp.zeros((n_slots,), jnp.int32).at[jnp.where(sel, slot, n_slots)].set(token, mode="drop")
    tile_expert = jnp.minimum(jnp.searchsorted(tile_end, jnp.arange(n_tiles, dtype=jnp.int32), side="right"),
                              N_EXPERTS - 1).astype(jnp.int32)
    n_used = tile_end[-1:].astype(jnp.int32)
    e1 = jnp.argmax(g8, axis=1)
    w1 = jnp.max(g8, axis=1)
    rest = jnp.where(jnp.arange(N_EXPERTS)[None, :] == e1[:, None], -1.0, g8)
    e2 = jnp.argmax(rest, axis=1)
    w2 = jnp.maximum(jnp.max(rest, axis=1), 0.0)
    slot1 = jnp.take_along_axis(slot, e1[:, None], axis=1)[:, 0]
    slot2 = jnp.where(w2 > 0.0, jnp.take_along_axis(slot, e2[:, None], axis=1)[:, 0], slot1)
    weights = jnp.zeros((m, LANES), F32).at[:, 0].set(w1).at[:, 1].set(w2)
    return token_of_slot, tile_expert, n_used, slot1.astype(jnp.int32), slot2.astype(jnp.int32), weights


def _row_copy(src_hbm, dst, src_row, r, sem):
    return pltpu.make_async_copy(src_hbm.at[pl.ds(src_row, 1), :], dst.at[pl.ds(r, 1), :], sem)


def _row_gather(src_hbm, gathers, base, n, sem):
    def start(r, c):
        for dst, idx_ref in gathers:
            _row_copy(src_hbm, dst, idx_ref[base + r], r, sem).start()
        return c

    def wait(r, c):
        for dst, _ in gathers:
            _row_copy(src_hbm, dst, 0, r, sem).wait()
        return c

    lax.fori_loop(0, n, start, 0)
    lax.fori_loop(0, n, wait, 0)


def _gather_body(tok_ref, used_ref, h_hbm, o_ref, buf, sem, *, rows, tm):
    i = pl.program_id(0)
    live = i * rows < used_ref[0] * tm

    @pl.when(live)
    def _():
        _row_gather(h_hbm, [(buf, tok_ref)], i * rows, rows, sem)
        o_ref[...] = buf[...].astype(o_ref.dtype)

    @pl.when(jnp.logical_not(live))
    def _():
        o_ref[...] = jnp.zeros_like(o_ref)


def _moe_gather(h32, token_of_slot, n_used, tm):
    d = h32.shape[1]
    n_slots = token_of_slot.shape[0]
    rows = ROW_BLOCK
    return pl.pallas_call(
        functools.partial(_gather_body, rows=rows, tm=tm),
        grid_spec=pltpu.PrefetchScalarGridSpec(
            num_scalar_prefetch=2,
            grid=(n_slots // rows,),
            in_specs=[pl.BlockSpec(memory_space=pl.ANY)],
            out_specs=pl.BlockSpec((rows, d), lambda i, tok, used: (i, 0)),
            scratch_shapes=[pltpu.VMEM((rows, d), F32), pltpu.SemaphoreType.DMA]),
        out_shape=jax.ShapeDtypeStruct((n_slots, d), BF16),
        compiler_params=_params("arbitrary"),
        name="moe_gather",
    )(token_of_slot, n_used, h32)


def _grouped_gated_body(te_ref, used_ref, a_ref, wg_ref, wu_ref, o_ref):
    @pl.when(pl.program_id(0) < used_ref[0])
    def _():
        a = a_ref[...]
        g = jnp.dot(a, wg_ref[...], preferred_element_type=F32)
        u = jnp.dot(a, wu_ref[...], preferred_element_type=F32)
        o_ref[...] = (g * jax.nn.sigmoid(g) * u).astype(o_ref.dtype)

    @pl.when(pl.program_id(0) >= used_ref[0])
    def _():
        o_ref[...] = jnp.zeros_like(o_ref)


def _grouped_swiglu(xs, w, first, tile_expert, n_used, *, tm, tn):
    s, kk = xs.shape
    f = w.shape[2] // 2
    nf = f // tn
    return pl.pallas_call(
        _grouped_gated_body,
        grid_spec=pltpu.PrefetchScalarGridSpec(
            num_scalar_prefetch=2,
            grid=(s // tm, nf),
            in_specs=[pl.BlockSpec((tm, kk), lambda i, j, te, used: (i, 0)),
                      pl.BlockSpec((None, kk, tn), lambda i, j, te, used: (first + te[i], 0, j)),
                      pl.BlockSpec((None, kk, tn), lambda i, j, te, used: (first + te[i], 0, nf + j))],
            out_specs=pl.BlockSpec((tm, tn), lambda i, j, te, used: (i, j))),
        out_shape=jax.ShapeDtypeStruct((s, f), BF16),
        compiler_params=_params("parallel", "parallel"),
        name="moe_grouped_swiglu",
    )(tile_expert, n_used, xs, w, w)


def _grouped_mm_body(te_ref, used_ref, a_ref, w_ref, o_ref):
    @pl.when(pl.program_id(0) < used_ref[0])
    def _():
        o_ref[...] = jnp.dot(a_ref[...], w_ref[...], preferred_element_type=F32).astype(o_ref.dtype)

    @pl.when(pl.program_id(0) >= used_ref[0])
    def _():
        o_ref[...] = jnp.zeros_like(o_ref)


def _grouped_matmul(a, w, first, tile_expert, n_used, *, tm, tn, out_dtype):
    s, kk = a.shape
    n = w.shape[2]
    return pl.pallas_call(
        _grouped_mm_body,
        grid_spec=pltpu.PrefetchScalarGridSpec(
            num_scalar_prefetch=2,
            grid=(s // tm, n // tn),
            in_specs=[pl.BlockSpec((tm, kk), lambda i, j, te, used: (i, 0)),
                      pl.BlockSpec((None, kk, tn), lambda i, j, te, used: (first + te[i], 0, j))],
            out_specs=pl.BlockSpec((tm, tn), lambda i, j, te, used: (i, j))),
        out_shape=jax.ShapeDtypeStruct((s, n), out_dtype),
        compiler_params=_params("parallel", "parallel"),
        name="moe_grouped_matmul",
    )(tile_expert, n_used, a, w)


def _post_moe_body(s1_ref, s2_ref, ys_hbm, x_ref, wts_ref, gp_ref, gate_ref, *rest, with_next, off):
    if with_next:
        gn_ref, sh_ref, sc_ref, xo_ref, h_ref, buf, sem = rest
    else:
        xo_ref, buf, sem = rest
    base = (pl.program_id(0) + off) * ROW_BLOCK
    _row_gather(ys_hbm, [(buf.at[0], s1_ref), (buf.at[1], s2_ref)], base, ROW_BLOCK, sem)
    wts = wts_ref[...]
    lane = lax.broadcasted_iota(jnp.int32, wts.shape, 1)
    w1 = jnp.sum(jnp.where(lane == 0, wts, 0.0), axis=1, keepdims=True)
    w2 = jnp.sum(jnp.where(lane == 1, wts, 0.0), axis=1, keepdims=True)
    y = w1 * buf[0] + w2 * buf[1]
    x_new = x_ref[...] + gate_ref[...] * _rms(y, gp_ref[...])
    xo_ref[...] = x_new
    if with_next:
        h = _rms(x_new, gn_ref[...]) * (1.0 + sc_ref[...]) + sh_ref[...]
        h_ref[...] = h.astype(h_ref.dtype)


def _postnorm_moe(ys, slot1, slot2, weights, x, g_post, gate, nxt=None, latent_only=False):
    m, d = x.shape
    nc = CTX_LEN // ROW_BLOCK
    off = nc if latent_only else 0
    n_ctx = 0 if latent_only else nc
    m_out = m - off * ROW_BLOCK
    row = lambda w: pl.BlockSpec((ROW_BLOCK, w), lambda i, s1, s2: (i + off, 0))
    vec = pl.BlockSpec((1, d), lambda i, s1, s2: (0, 0))
    modv = pl.BlockSpec((None, 1, d), lambda i, s1, s2: (jnp.where(i < n_ctx, 0, 1), 0, 0))
    out_row = lambda: pl.BlockSpec((ROW_BLOCK, d), lambda i, s1, s2: (i, 0))
    ins = [ys, x, weights, g_post.reshape(1, d), gate]
    in_specs = [pl.BlockSpec(memory_space=pl.ANY), row(d), row(LANES), vec, modv]
    out_shape = [jax.ShapeDtypeStruct((m_out, d), F32)]
    out_specs = [out_row()]
    if nxt is not None:
        ins += [nxt[0].reshape(1, d), nxt[1], nxt[2]]
        in_specs += [vec, modv, modv]
        out_shape.append(jax.ShapeDtypeStruct((m_out, d), BF16))
        out_specs.append(out_row())
    return pl.pallas_call(
        functools.partial(_post_moe_body, with_next=nxt is not None, off=off),
        grid_spec=pltpu.PrefetchScalarGridSpec(
            num_scalar_prefetch=2,
            grid=(m_out // ROW_BLOCK,),
            in_specs=in_specs,
            out_specs=out_specs,
            scratch_shapes=[pltpu.VMEM((2, ROW_BLOCK, d), F32), pltpu.SemaphoreType.DMA]),
        out_shape=out_shape,
        compiler_params=_params("arbitrary"),
        name="postnorm_moe",
    )(slot1, slot2, *ins)


def _s5_lag_body(b_ref, c_ref, o_ref):
    for g in range(b_ref.shape[0]):
        o_ref[g] = jnp.dot(b_ref[g], c_ref[g], preferred_element_type=F32,
                           precision=lax.Precision.HIGHEST)


def _s5_lag_kernels(b_lag, c_cat):
    g, r, n2 = b_lag.shape
    gb = S5_SLAB_GROUPS
    return pl.pallas_call(
        _s5_lag_body,
        grid=(g // gb,),
        in_specs=[pl.BlockSpec((gb, r, n2), lambda i: (i, 0, 0)),
                  pl.BlockSpec((gb, n2, S5_GROUP), lambda i: (i, 0, 0))],
        out_specs=pl.BlockSpec((gb, r, S5_GROUP), lambda i: (i, 0, 0)),
        out_shape=jax.ShapeDtypeStruct((g, r, S5_GROUP), F32),
        compiler_params=_params("parallel"),
        name="s5_lag_kernels",
    )(b_lag, c_cat)


def _chunk_tile(nc):
    return S5_CHUNK_TILE if nc % S5_CHUNK_TILE == 0 else nc


def _slab_rows(g, k):
    return pl.ds(k * LANES + g * S5_GROUP, S5_GROUP)


def _s5_in_body(u_ref, bc_ref, vre_ref, vim_ref, w_ref, *, tc):
    first = jnp.logical_and(pl.program_id(0) == 0, pl.program_id(1) == 0)

    @pl.when(first)
    def _():
        w_ref[...] = jnp.zeros_like(w_ref)

    @pl.when(pl.program_id(1) == 0)
    def _():
        for g in range(S5_SLAB_GROUPS):
            for k in range(S5_CHUNK):
                w_ref[g, _slab_rows(g, k), :] = bc_ref[g, k * S5_GROUP:(k + 1) * S5_GROUP, :]

    u = u_ref[...]
    for g in range(S5_SLAB_GROUPS):
        v = jnp.dot(u, w_ref[g], preferred_element_type=F32)
        rows = pl.ds(g, tc, stride=S5_SLAB_GROUPS)
        vre_ref[rows, :] = v[:, :LANES]
        vim_ref[rows, :] = v[:, LANES:]


def _s5_state_inputs(u, bc):
    ns, nc, sw = u.shape
    tc = _chunk_tile(nc)
    gb = S5_SLAB_GROUPS
    n4 = 4 * S5_STATE
    out = pl.BlockSpec((None, tc * gb, LANES), lambda s, m: (s, m, 0))
    return pl.pallas_call(
        functools.partial(_s5_in_body, tc=tc),
        grid=(ns, nc // tc),
        in_specs=[pl.BlockSpec((None, tc, sw), lambda s, m: (s, m, 0)),
                  pl.BlockSpec((gb, S5_CK, n4), lambda s, m: (s, 0, 0))],
        out_specs=[out, out],
        out_shape=[jax.ShapeDtypeStruct((ns, nc * gb, LANES), F32)] * 2,
        scratch_shapes=[pltpu.VMEM((gb, sw, n4), BF16)],
        compiler_params=_params("arbitrary", "arbitrary"),
        name="s5_state_inputs",
    )(u, bc)


def _s5_scan_body(vre_ref, vim_ref, a_ref, sre_ref, sim_ref, *, nc, nctx):
    gb = S5_SLAB_GROUPS
    a_re, a_im = a_ref[:, :LANES], a_ref[:, LANES:]
    fwd = lax.broadcasted_iota(jnp.int32, (gb, LANES), 1) < S5_STATE
    bwd = jnp.logical_not(fwd)

    def step(t, carry):
        s_re, s_im = carry
        t_b = jnp.where(t < nctx, nctx - 1 - t, nc - 1 - (t - nctx))
        rf = pl.ds(pl.multiple_of(t * gb, gb), gb)
        rb = pl.ds(pl.multiple_of(t_b * gb, gb), gb)
        pltpu.store(sre_ref.at[rf, :], s_re, mask=fwd)
        pltpu.store(sre_ref.at[rb, :], s_re, mask=bwd)
        pltpu.store(sim_ref.at[rf, :], s_im, mask=fwd)
        pltpu.store(sim_ref.at[rb, :], s_im, mask=bwd)
        v_re = jnp.where(fwd, vre_ref[rf, :], vre_ref[rb, :])
        v_im = jnp.where(fwd, vim_ref[rf, :], vim_ref[rb, :])
        return a_re * s_re - a_im * s_im + v_re, a_re * s_im + a_im * s_re + v_im

    zero = jnp.zeros((gb, LANES), F32)
    lax.fori_loop(0, nc, step, (zero, zero))


def _s5_scan(v_re, v_im, a_chunk):
    ns, rows, _ = v_re.shape
    gb = S5_SLAB_GROUPS
    nc = rows // gb
    blk = pl.BlockSpec((None, rows, LANES), lambda s: (s, 0, 0))
    return pl.pallas_call(
        functools.partial(_s5_scan_body, nc=nc, nctx=CTX_LEN // S5_CHUNK),
        grid=(ns,),
        in_specs=[blk, blk, pl.BlockSpec((None, gb, 2 * LANES), lambda s: (s, 0, 0))],
        out_specs=[blk, blk],
        out_shape=[jax.ShapeDtypeStruct((ns, rows, LANES), F32)] * 2,
        compiler_params=_params("parallel"),
        name="s5_scan",
    )(v_re, v_im, a_chunk)


def _s5_out_body(u_ref, sre_ref, sim_ref, t_ref, c_ref, y_ref, wt_ref, wc_ref, scat_ref, *, tc):
    gb = S5_SLAB_GROUPS
    n4 = 4 * S5_STATE

    @pl.when(pl.program_id(1) == 0)
    def _():
        row = lax.broadcasted_iota(jnp.int32, (S5_CK, S5_SLAB_W), 0)
        col = lax.broadcasted_iota(jnp.int32, (S5_CK, S5_SLAB_W), 1)
        spread = (row // S5_GROUP) * LANES + row % S5_GROUP
        for g in range(gb):
            place = (col == spread + g * S5_GROUP).astype(BF16)
            tg = jnp.dot(t_ref[g], place, preferred_element_type=F32).astype(BF16)
            for k in range(S5_CHUNK):
                wt_ref[_slab_rows(g, k), :] = tg[k * S5_GROUP:(k + 1) * S5_GROUP, :]
            wc_ref[g * n4:(g + 1) * n4, :] = jnp.dot(c_ref[g], place, preferred_element_type=F32).astype(BF16)

    for g in range(gb):
        rows = pl.ds(g, tc, stride=gb)
        scat_ref[:, g * n4:g * n4 + LANES] = sre_ref[rows, :].astype(BF16)
        scat_ref[:, g * n4 + LANES:(g + 1) * n4] = sim_ref[rows, :].astype(BF16)
    y = (jnp.dot(u_ref[...], wt_ref[...], preferred_element_type=F32)
         + jnp.dot(scat_ref[...], wc_ref[...], preferred_element_type=F32))
    y_ref[...] = jax.nn.gelu(y).astype(y_ref.dtype)


def _s5_outputs(u, s_re, s_im, toep, cc):
    ns, nc, sw = u.shape
    tc = _chunk_tile(nc)
    gb = S5_SLAB_GROUPS
    n4 = 4 * S5_STATE
    state = pl.BlockSpec((None, tc * gb, LANES), lambda i, m: (i, m, 0))
    return pl.pallas_call(
        functools.partial(_s5_out_body, tc=tc),
        grid=(ns, nc // tc),
        in_specs=[pl.BlockSpec((None, tc, sw), lambda i, m: (i, m, 0)), state, state,
                  pl.BlockSpec((gb, S5_CK, S5_CK), lambda i, m: (i, 0, 0)),
                  pl.BlockSpec((gb, n4, S5_CK), lambda i, m: (i, 0, 0))],
        out_specs=pl.BlockSpec((None, tc, sw), lambda i, m: (i, m, 0)),
        out_shape=jax.ShapeDtypeStruct((ns, nc, sw), BF16),
        scratch_shapes=[pltpu.VMEM((sw, sw), BF16), pltpu.VMEM((gb * n4, sw), BF16),
                        pltpu.VMEM((tc, gb * n4), BF16)],
        compiler_params=_params("parallel", "arbitrary"),
        name="s5_outputs",
    )(u, s_re, s_im, toep, cc)


def _s5_operators(a_re, a_im, log_dt, b_re, b_im, c_re, c_im, d_skip):
    L, G, N, C = S5_CHUNK, S5_GROUPS, S5_STATE, S5_GROUP
    a_re, a_im = a_re.astype(F32), a_im.astype(F32)
    b_re, b_im = b_re.astype(F32), b_im.astype(F32)
    c_re, c_im = c_re.astype(F32), c_im.astype(F32)
    dt = jnp.exp(log_dt.astype(F32))[..., None]
    steps = jnp.arange(L + 1, dtype=F32)[:, None, None, None]
    mag = jnp.exp((a_re * dt)[None] * steps)
    ang = (a_im * dt)[None] * steps
    pw_re, pw_im = mag * jnp.cos(ang), mag * jnp.sin(ang)
    p, q = pw_re[1] - 1.0, pw_im[1]
    den = a_re * a_re + a_im * a_im
    f_re, f_im = (p * a_re + q * a_im) / den, (q * a_re - p * a_im) / den
    bb_re = f_re[..., None] * b_re - f_im[..., None] * b_im
    bb_im = f_re[..., None] * b_im + f_im[..., None] * b_re

    def cmul(xr, xi, yr, yi):
        return xr * yr - xi * yi, xr * yi + xi * yr

    bl_re, bl_im = cmul(pw_re[:L, :, :, :, None], pw_im[:L, :, :, :, None], bb_re[None], bb_im[None])
    lag_rows = lambda z: jnp.transpose(z, (1, 2, 0, 4, 3)).reshape(2 * G, L * C, N)
    b_lag = jnp.concatenate([lag_rows(bl_re), -lag_rows(bl_im)], axis=-1)
    c_t = lambda z: jnp.swapaxes(z, -1, -2).reshape(2 * G, N, C)
    c_cat = jnp.concatenate([c_t(c_re), c_t(c_im)], axis=1)
    k_lag = _s5_lag_kernels(b_lag, c_cat).reshape(2, G, L, C, C)

    k_idx = jnp.arange(L)
    lag = k_idx[None, :] - k_idx[:, None]
    kf = jnp.where((lag >= 0)[None, :, :, None, None], k_lag[0][:, jnp.clip(lag, 0, L - 1)], 0.0)
    kr = jnp.where((lag <= 0)[None, :, :, None, None], k_lag[1][:, jnp.clip(-lag, 0, L - 1)], 0.0)
    eye_k = jnp.eye(L, dtype=F32)[None, :, :, None, None]
    d_diag = d_skip.astype(F32).reshape(G, 1, 1, C, 1) * jnp.eye(C, dtype=F32)[None, None, None]
    toep = kf + kr + eye_k * d_diag
    toep = jnp.transpose(toep, (0, 1, 3, 2, 4)).reshape(G, L * C, L * C)

    bf_re, bf_im = cmul(pw_re[L - 1 - k_idx, 0][..., None], pw_im[L - 1 - k_idx, 0][..., None],
                        bb_re[0][None], bb_im[0][None])
    bk_re, bk_im = cmul(pw_re[k_idx, 1][..., None], pw_im[k_idx, 1][..., None],
                        bb_re[1][None], bb_im[1][None])
    to_rows = lambda z: jnp.transpose(z, (1, 0, 3, 2)).reshape(G, L * C, N)
    bc = jnp.concatenate([to_rows(bf_re), to_rows(bk_re), to_rows(bf_im), to_rows(bk_im)], axis=-1)

    pw_g = lambda z, idx, d: jnp.transpose(z[idx, d], (1, 0, 2))[:, :, None, :]
    cf_re, cf_im = cmul(c_re[0][:, None], c_im[0][:, None], pw_g(pw_re, k_idx + 1, 0), pw_g(pw_im, k_idx + 1, 0))
    cb_re, cb_im = cmul(c_re[1][:, None], c_im[1][:, None], pw_g(pw_re, L - k_idx, 1), pw_g(pw_im, L - k_idx, 1))
    to_cols = lambda z: jnp.transpose(z, (0, 3, 1, 2)).reshape(G, N, L * C)
    cc = jnp.concatenate([to_cols(cf_re), to_cols(cb_re), -to_cols(cf_im), -to_cols(cb_im)], axis=1)

    a_chunk = jnp.concatenate([pw_re[L, 0], pw_re[L, 1], pw_im[L, 0], pw_im[L, 1]], axis=-1)
    a_chunk = a_chunk.reshape(S5_SLABS, S5_SLAB_GROUPS, 4 * N)
    return toep.astype(BF16), bc.astype(BF16), cc.astype(BF16), a_chunk


def _s5_mixer(h, ops):
    toep, bc, cc, a_chunk = ops
    m, d = h.shape
    nc = m // S5_CHUNK
    u = jnp.transpose(h.reshape(nc, S5_CHUNK, S5_SLABS, LANES), (2, 0, 1, 3)).reshape(S5_SLABS, nc, S5_SLAB_W)
    v_re, v_im = _s5_state_inputs(u, bc)
    s_re, s_im = _s5_scan(v_re, v_im, a_chunk)
    y = _s5_outputs(u, s_re, s_im, toep, cc)
    return jnp.transpose(y.reshape(S5_SLABS, nc, S5_CHUNK, LANES), (1, 2, 0, 3)).reshape(m, d)


def _na_bias_tables(rpb):
    var = jnp.arange(NA_KH)
    row_rel = jnp.arange(NA_KH)[None, :] - var[:, None] + (NA_KH - 1)
    q = jnp.arange(GRID_W)
    col_start = jnp.clip(q - NA_KW // 2, 0, GRID_W - NA_KW)
    kc = jnp.arange(GRID_W)
    inside = (kc[None, :] >= col_start[:, None]) & (kc[None, :] < col_start[:, None] + NA_KW)
    col_rel = jnp.clip(kc[None, :] - q[:, None] + (NA_KW - 1), 0, 2 * NA_KW - 2)
    t = rpb.astype(F32)[:, row_rel][:, :, :, col_rel]
    t = jnp.where(inside[None, None, None], t, NA_NEG)
    t = jnp.transpose(t, (1, 0, 3, 2, 4)).reshape(NA_KH, rpb.shape[0], GRID_W, NA_KH * GRID_W)

    def step_table(variants, offsets):
        rows = [jnp.pad(t[v], ((0, 0), (0, 0), (o * GRID_W, (NA_BAND - NA_KH - o) * GRID_W)),
                        constant_values=NA_NEG) for v, o in zip(variants, offsets)]
        return jnp.concatenate(rows, axis=1)

    r = NA_ROWS_PER_STEP
    half = NA_KH // 2
    first = step_table(range(r), [0] * r)
    inner = step_table([half] * r, range(r))
    last = step_table(range(half, half + r), [NA_BAND - NA_KH] * r)
    return jnp.stack([first, inner, last]) * LOG2_E


def _na_body(q_ref, k_ref, v_ref, bias_ref, o_ref, *, rows):
    i = pl.program_id(1)
    dn = (((1,), (1,)), ((), ()))
    nctx = CTX_LEN // (NA_ROWS_PER_STEP * GRID_W)
    band = NA_BAND * GRID_W
    heads = [pl.ds(hd * NA_HEAD_DIM, NA_HEAD_DIM) for hd in range(NA_HEADS_PER_STEP)]

    def ctx_scores(hd):
        return lax.dot_general(q_ref[:, hd], k_ref[0:CTX_LEN, hd], dn, preferred_element_type=F32)

    @pl.when(i < nctx)
    def _():
        for hd in heads:
            s_c = ctx_scores(hd)
            p = jnp.exp2(s_c - jnp.max(s_c, axis=-1, keepdims=True))
            o = jnp.dot(p.astype(BF16), v_ref[0:CTX_LEN, hd], preferred_element_type=F32)
            o_ref[:, hd] = (o / jnp.sum(p, axis=-1, keepdims=True)).astype(o_ref.dtype)

    @pl.when(i >= nctx)
    def _():
        r = (i - nctx) * NA_ROWS_PER_STEP
        rb = jnp.clip(r - NA_KH // 2, 0, rows - NA_BAND)
        kind = jnp.where(r == 0, 0, jnp.where(r == rows - NA_ROWS_PER_STEP, 2, 1))
        start = pl.multiple_of(CTX_LEN + rb * GRID_W, GRID_W)
        for n, hd in enumerate(heads):
            s_c = ctx_scores(hd)
            s_w = lax.dot_general(q_ref[:, hd], k_ref[pl.ds(start, band), hd], dn,
                                  preferred_element_type=F32) + bias_ref[kind, n]
            mx = jnp.maximum(jnp.max(s_w, axis=-1, keepdims=True), jnp.max(s_c, axis=-1, keepdims=True))
            p_w = jnp.exp2(s_w - mx)
            p_c = jnp.exp2(s_c - mx)
            den = jnp.sum(p_w, axis=-1, keepdims=True) + jnp.sum(p_c, axis=-1, keepdims=True)
            o = (jnp.dot(p_w.astype(BF16), v_ref[pl.ds(start, band), hd], preferred_element_type=F32)
                 + jnp.dot(p_c.astype(BF16), v_ref[0:CTX_LEN, hd], preferred_element_type=F32))
            o_ref[:, hd] = (o / den).astype(o_ref.dtype)


def _na_attention(qkv, bias):
    m = qkv.shape[0]
    rows = (m - CTX_LEN) // GRID_W
    assert rows % NA_ROWS_PER_STEP == 0 and rows >= NA_BAND + NA_ROWS_PER_STEP
    tq = NA_ROWS_PER_STEP * GRID_W
    hs = NA_HEADS_PER_STEP
    w = hs * NA_HEAD_DIM
    nh = NA_HEADS // hs
    return pl.pallas_call(
        functools.partial(_na_body, rows=rows),
        grid=(nh, m // tq),
        in_specs=[pl.BlockSpec((tq, w), lambda h, i: (i, h)),
                  pl.BlockSpec((m, w), lambda h, i: (0, nh + h)),
                  pl.BlockSpec((m, w), lambda h, i: (0, 2 * nh + h)),
                  pl.BlockSpec((3, hs, tq, NA_BAND * GRID_W), lambda h, i: (0, h, 0, 0))],
        out_specs=pl.BlockSpec((tq, w), lambda h, i: (i, h)),
        out_shape=jax.ShapeDtypeStruct((m, D_MODEL), BF16),
        compiler_params=_params("parallel", "arbitrary"),
        name="na_attention",
    )(qkv, qkv, qkv, bias)


def _tiles(m):
    for blocks in range(5, 0, -1):
        if m % (blocks * ROW_BLOCK) == 0:
            return blocks * ROW_BLOCK
    raise ValueError(m)


def kernel(x, c, ctx, c_ctx, ada_w, ada_b, norm_g, s5_a_re, s5_a_im, s5_log_dt, s5_b_re, s5_b_im,
           s5_c_re, s5_c_im, s5_d, s5_w_glu, na_w_qkv, na_w_o, na_rpb, ffn_w_gu, ffn_w_down,
           moe_w_router, moe_w_gu, moe_w_down):
    assert x.shape[0] == 1 and x.shape[2] == D_MODEL and ctx.shape[1] == CTX_LEN
    d = D_MODEL
    xc = jnp.concatenate([ctx[0], x[0]], axis=0).astype(F32)
    m = xc.shape[0]
    tm = _tiles(m)

    cond_t = jnp.stack([c_ctx, c[0]], axis=1).astype(F32)
    mods = _adaln(cond_t, ada_w.astype(F32), ada_b.astype(F32))
    mods = mods.reshape(DEPTH, 2, 6, 1, d)
    mod = lambda i, k: mods[i, :, k]

    w_glu, w_ffn_gu, w_ffn_down = s5_w_glu.astype(BF16), ffn_w_gu.astype(BF16), ffn_w_down.astype(BF16)
    w_qkv, w_o = na_w_qkv.astype(BF16), na_w_o.astype(BF16)
    n_odd = moe_w_gu.shape[0]
    w_moe_gu = moe_w_gu.astype(BF16).reshape(n_odd * N_EXPERTS, d, 2 * EXPERT_DIM)
    w_moe_down = moe_w_down.astype(BF16).reshape(n_odd * N_EXPERTS, EXPERT_DIM, d)

    h = _prenorm(xc, norm_g[0, 0], mod(0, 0), mod(0, 1))
    for i in range(DEPTH):
        j = i // 2
        last = i == DEPTH - 1
        nxt_f = (norm_g[i, 2], mod(i, 3), mod(i, 4))
        if i % 2 == 0:
            ops = _s5_operators(s5_a_re[j], s5_a_im[j], s5_log_dt[j], s5_b_re[j], s5_b_im[j],
                                s5_c_re[j], s5_c_im[j], s5_d[j])
            g = _s5_mixer(h, ops)
            y = _gated_matmul(g, w_glu, j, 1, act="glu", tm=tm, tn=512, out_dtype=BF16)
            xc, h = _postnorm(y, xc, norm_g[i, 1], mod(i, 2), nxt_f)
            a = _gated_matmul(h, w_ffn_gu, j, 1, act="swiglu", tm=tm, tn=512, out_dtype=BF16)
            y = _matmul(a, w_ffn_down, j, tm=tm, tn=1024, out_dtype=BF16)
        else:
            qkv = _matmul(h, w_qkv, j, tm=tm, tn=1024, out_dtype=BF16, scaled_cols=d, scale=NA_Q_SCALE)
            o = _na_attention(qkv, _na_bias_tables(na_rpb[j]))
            y = _matmul(o, w_o, j, tm=tm, tn=1024, out_dtype=BF16)
            xc, h32, gates = _postnorm(y, xc, norm_g[i, 1], mod(i, 2), nxt_f, w_router=moe_w_router[j],
                                       h_dtype=F32)
            token_of_slot, tile_expert, n_used, slot1, slot2, weights = _moe_routing(gates, tm)
            xs = _moe_gather(h32, token_of_slot, n_used, tm)
            a = _grouped_swiglu(xs, w_moe_gu, j * N_EXPERTS, tile_expert, n_used, tm=tm, tn=512)
            ys = _grouped_matmul(a, w_moe_down, j * N_EXPERTS, tile_expert, n_used, tm=tm, tn=2048,
                                 out_dtype=F32)
            if last:
                out, = _postnorm_moe(ys, slot1, slot2, weights, xc, norm_g[i, 3], mod(i, 5), latent_only=True)
                return out[None].astype(x.dtype)
            nxt_m = (norm_g[i + 1, 0], mod(i + 1, 0), mod(i + 1, 1))
            xc, h = _postnorm_moe(ys, slot1, slot2, weights, xc, norm_g[i, 3], mod(i, 5), nxt_m)
            continue
        nxt_m = (norm_g[i + 1, 0], mod(i + 1, 0), mod(i + 1, 1))
        xc, h = _postnorm(y, xc, norm_g[i, 3], mod(i, 5), nxt_m)
```

```python
import functools

import jax
import jax.numpy as jnp
from jax import lax
from jax.experimental import pallas as pl
from jax.experimental.pallas import tpu as pltpu

D_MODEL = 4096
DEPTH = 4
GRID_W = 64
CTX_LEN = 256
NORM_EPS = 1e-6
S5_GROUP = 16
S5_GROUPS = D_MODEL // S5_GROUP
S5_STATE = 64
NA_HEADS = 32
NA_HEAD_DIM = D_MODEL // NA_HEADS
NA_KH = 8
NA_KW = 16
N_EXPERTS = 8
EXPERT_DIM = 1024

LANES = 128
S5_CHUNK = 16
S5_CK = S5_CHUNK * S5_GROUP
S5_SLAB_GROUPS = LANES // S5_GROUP
S5_SLABS = S5_GROUPS // S5_SLAB_GROUPS
S5_SLAB_W = S5_CHUNK * LANES
S5_CHUNK_TILE = 520
NA_ROWS_PER_STEP = 4
NA_HEADS_PER_STEP = 2
NA_BAND = NA_KH + NA_ROWS_PER_STEP
NA_NEG = -1e30
LOG2_E = 1.4426950408889634
NA_Q_SCALE = NA_HEAD_DIM ** -0.5 * LOG2_E
ROW_BLOCK = 256
VMEM_LIMIT_V7X = 56 * 1024 * 1024

BF16 = jnp.bfloat16
F32 = jnp.float32


def _params(*sem):
    return pltpu.CompilerParams(dimension_semantics=sem, vmem_limit_bytes=VMEM_LIMIT_V7X)


def _adaln_body(cond_ref, w_ref, b_ref, o_ref):
    cond = cond_ref[...]
    s = cond * jax.nn.sigmoid(cond)
    w = w_ref[...]
    r0 = jnp.sum(w * s[:, 0:1], axis=0, keepdims=True)
    r1 = jnp.sum(w * s[:, 1:2], axis=0, keepdims=True)
    o_ref[...] = jnp.concatenate([r0, r1], axis=0) + b_ref[...]


def _adaln(cond_t, ada_w, ada_b, tn=512):
    depth, d, n = ada_w.shape
    return pl.pallas_call(
        _adaln_body,
        grid=(depth, n // tn),
        in_specs=[
            pl.BlockSpec((d, 2), lambda l, j: (0, 0)),
            pl.BlockSpec((None, d, tn), lambda l, j: (l, 0, j)),
            pl.BlockSpec((None, 1, tn), lambda l, j: (l, 0, j)),
        ],
        out_specs=pl.BlockSpec((None, 2, tn), lambda l, j: (l, 0, j)),
        out_shape=jax.ShapeDtypeStruct((depth, 2, n), F32),
        compiler_params=_params("parallel", "parallel"),
        name="adaln",
    )(cond_t, ada_w, ada_b.reshape(depth, 1, n))


def _rms(x, g):
    return x * lax.rsqrt(jnp.mean(x * x, axis=-1, keepdims=True) + NORM_EPS) * g


def _route(h, wr_ref, gates_ref):
    logits = jnp.dot(h, wr_ref[...], preferred_element_type=F32, precision=lax.Precision.HIGHEST)
    lane = lax.broadcasted_iota(jnp.int32, logits.shape, 1).astype(F32)
    n_lanes = float(logits.shape[1])
    logits = jnp.where(lane < N_EXPERTS, logits, -jnp.inf)
    m1 = jnp.max(logits, axis=-1, keepdims=True)
    i1 = jnp.min(jnp.where(logits == m1, lane, n_lanes), axis=-1, keepdims=True)
    rest = jnp.where(lane == i1, -jnp.inf, logits)
    m2 = jnp.max(rest, axis=-1, keepdims=True)
    i2 = jnp.min(jnp.where(rest == m2, lane, n_lanes), axis=-1, keepdims=True)
    e2 = jnp.exp(m2 - m1)
    w1 = 1.0 / (1.0 + e2)
    w2 = e2 / (1.0 + e2)
    gates_ref[...] = jnp.where(lane == i1, w1, 0.0) + jnp.where(lane == i2, w2, 0.0)


def _pre_body(x_ref, g_ref, sh_ref, sc_ref, h_ref):
    h = _rms(x_ref[...], g_ref[...]) * (1.0 + sc_ref[...]) + sh_ref[...]
    h_ref[...] = h.astype(h_ref.dtype)


def _post_body(*refs, with_next, with_router):
    y_ref, x_ref, gp_ref, gate_ref = refs[:4]
    x_new = x_ref[...] + gate_ref[...] * _rms(y_ref[...].astype(F32), gp_ref[...])
    if not with_next:
        refs[4][...] = x_new
        return
    gn_ref, sh_ref, sc_ref = refs[4:7]
    rest = refs[7:]
    if with_router:
        wr_ref, rest = rest[0], rest[1:]
    xo_ref, h_ref = rest[:2]
    xo_ref[...] = x_new
    h = _rms(x_new, gn_ref[...]) * (1.0 + sc_ref[...]) + sh_ref[...]
    h_ref[...] = h.astype(h_ref.dtype)
    if with_router:
        _route(h, wr_ref, rest[2])


def _row_spec(d, off=0):
    return pl.BlockSpec((ROW_BLOCK, d), lambda i: (i + off, 0))


def _vec_spec(d):
    return pl.BlockSpec((1, d), lambda i: (0, 0))


def _mod_spec(d, n_ctx_blocks):
    return pl.BlockSpec((None, 1, d), lambda i: (jnp.where(i < n_ctx_blocks, 0, 1), 0, 0))


def _prenorm(x, g, shift, scale):
    m, d = x.shape
    nc = CTX_LEN // ROW_BLOCK
    return pl.pallas_call(
        _pre_body,
        grid=(m // ROW_BLOCK,),
        in_specs=[_row_spec(d), _vec_spec(d), _mod_spec(d, nc), _mod_spec(d, nc)],
        out_specs=_row_spec(d),
        out_shape=jax.ShapeDtypeStruct((m, d), BF16),
        compiler_params=_params("parallel"),
        name="prenorm",
    )(x, g.reshape(1, d), shift, scale)


def _postnorm(y, x, g_post, gate, nxt=None, w_router=None, latent_only=False):
    m, d = x.shape
    nc = CTX_LEN // ROW_BLOCK
    off = nc if latent_only else 0
    n_ctx = 0 if latent_only else nc
    m_out = m - off * ROW_BLOCK
    ins = [y, x, g_post.reshape(1, d), gate]
    in_specs = [_row_spec(d, off), _row_spec(d, off), _vec_spec(d), _mod_spec(d, n_ctx)]
    out_shape = [jax.ShapeDtypeStruct((m_out, d), F32)]
    out_specs = [_row_spec(d)]
    if nxt is not None:
        ins += [nxt[0].reshape(1, d), nxt[1], nxt[2]]
        in_specs += [_vec_spec(d), _mod_spec(d, n_ctx), _mod_spec(d, n_ctx)]
        out_shape.append(jax.ShapeDtypeStruct((m_out, d), BF16))
        out_specs.append(_row_spec(d))
        if w_router is not None:
            wr = jnp.zeros((d, LANES), F32).at[:, :N_EXPERTS].set(w_router)
            ins.append(wr)
            in_specs.append(pl.BlockSpec((d, LANES), lambda i: (0, 0)))
            out_shape.append(jax.ShapeDtypeStruct((m_out, LANES), F32))
            out_specs.append(pl.BlockSpec((ROW_BLOCK, LANES), lambda i: (i, 0)))
    out = pl.pallas_call(
        functools.partial(_post_body, with_next=nxt is not None, with_router=w_router is not None),
        grid=(m_out // ROW_BLOCK,),
        in_specs=in_specs,
        out_specs=out_specs,
        out_shape=out_shape,
        compiler_params=_params("parallel"),
        name="postnorm",
    )(*ins)
    return out


def _mm_body(a_ref, w_ref, o_ref, *, scaled_tiles, scale):
    acc = jnp.dot(a_ref[...], w_ref[...], preferred_element_type=F32)
    if scaled_tiles:
        acc = acc * jnp.where(pl.program_id(1) < scaled_tiles, scale, 1.0)
    o_ref[...] = acc.astype(o_ref.dtype)


def _matmul(a, w, layer, *, tm, tn, out_dtype, scaled_cols=0, scale=1.0):
    m, kk = a.shape
    n = w.shape[2]
    assert scaled_cols % tn == 0
    return pl.pallas_call(
        functools.partial(_mm_body, scaled_tiles=scaled_cols // tn, scale=scale),
        grid=(m // tm, n // tn),
        in_specs=[pl.BlockSpec((tm, kk), lambda i, j: (i, 0)),
                  pl.BlockSpec((None, kk, tn), lambda i, j: (layer, 0, j))],
        out_specs=pl.BlockSpec((tm, tn), lambda i, j: (i, j)),
        out_shape=jax.ShapeDtypeStruct((m, n), out_dtype),
        compiler_params=_params("parallel", "parallel"),
        name="matmul",
    )(a, w)


def _gated_body(a_ref, wg_ref, wu_ref, o_ref, *, act):
    a = a_ref[...]
    g = jnp.dot(a, wg_ref[...], preferred_element_type=F32)
    u = jnp.dot(a, wu_ref[...], preferred_element_type=F32)
    if act == "swiglu":
        r = g * jax.nn.sigmoid(g) * u
    else:
        r = g * jax.nn.sigmoid(u)
    o_ref[...] = r.astype(o_ref.dtype)


def _gated_matmul(a, w, layer, *, act, tm, tn, out_dtype):
    m, kk = a.shape
    f = w.shape[2] // 2
    nf = f // tn
    return pl.pallas_call(
        functools.partial(_gated_body, act=act),
        grid=(m // tm, nf),
        in_specs=[pl.BlockSpec((tm, kk), lambda i, j: (i, 0)),
                  pl.BlockSpec((None, kk, tn), lambda i, j: (layer, 0, j)),
                  pl.BlockSpec((None, kk, tn), lambda i, j: (layer, 0, nf + j))],
        out_specs=pl.BlockSpec((tm, tn), lambda i, j: (i, j)),
        out_shape=jax.ShapeDtypeStruct((m, f), out_dtype),
        compiler_params=_params("parallel", "parallel"),
        name="gated_matmul",
    )(a, w, w)


def _moe_routing(gates, tm):
    m = gates.shape[0]
    g8 = gates[:, :N_EXPERTS]
    sel = g8 > 0.0
    pos = jnp.cumsum(sel.astype(jnp.int32), axis=0) - 1
    tiles_e = (pos[-1] + tm) // tm
    tile_end = jnp.cumsum(tiles_e)
    slot = ((tile_end - tiles_e) * tm)[None, :] + pos
    n_tiles = (2 * m + tm - 1) // tm + N_EXPERTS
    n_slots = n_tiles * tm
    tile_expert = jnp.minimum(jnp.searchsorted(tile_end, jnp.arange(n_tiles, dtype=jnp.int32), side="right"),
                              N_EXPERTS - 1).astype(jnp.int32)
    n_used = tile_end[-1:].astype(jnp.int32)
    e1 = jnp.argmax(g8, axis=1)
    w1 = jnp.max(g8, axis=1)
    rest = jnp.where(jnp.arange(N_EXPERTS)[None, :] == e1[:, None], -1.0, g8)
    e2 = jnp.argmax(rest, axis=1)
    w2 = jnp.maximum(jnp.max(rest, axis=1), 0.0)
    slot1 = jnp.take_along_axis(slot, e1[:, None], axis=1)[:, 0].astype(jnp.int32)
    slot2 = jnp.take_along_axis(slot, e2[:, None], axis=1)[:, 0].astype(jnp.int32)
    two = w2 > 0.0
    weights = jnp.zeros((m, LANES), F32).at[:, 0].set(w1).at[:, 1].set(w2)
    return (tile_expert, n_used, slot1, jnp.where(two, slot2, n_slots), jnp.where(two, slot2, slot1), weights,
            n_slots)


def _scatter_body(s1_ref, s2_ref, h_ref, xs_in, xs_out, buf, sem):
    del xs_in
    base = pl.program_id(0) * ROW_BLOCK
    bits = lax.bitcast_convert_type(h_ref[...].astype(F32), jnp.uint32)
    half = bits.shape[1] // 2
    buf[...] = bits[:, :half] | (bits[:, half:] >> 16)

    def copies(r, row1, row2):
        return (pltpu.make_async_copy(buf.at[pl.ds(r, 1), :], xs_out.at[pl.ds(row1, 1), :], sem),
                pltpu.make_async_copy(buf.at[pl.ds(r, 1), :], xs_out.at[pl.ds(row2, 1), :], sem))

    def start(r, c):
        for cp in copies(r, s1_ref[base + r], s2_ref[base + r]):
            cp.start()
        return c

    def wait(r, c):
        for cp in copies(r, 0, 0):
            cp.wait()
        return c

    lax.fori_loop(0, ROW_BLOCK, start, 0)
    lax.fori_loop(0, ROW_BLOCK, wait, 0)


def _moe_scatter(h, slot1, slot2, n_slots):
    m, d = h.shape
    spare = 8
    xs0 = jnp.zeros((n_slots + spare, d // 2), jnp.uint32)
    return pl.pallas_call(
        _scatter_body,
        grid_spec=pltpu.PrefetchScalarGridSpec(
            num_scalar_prefetch=2,
            grid=(m // ROW_BLOCK,),
            in_specs=[pl.BlockSpec((ROW_BLOCK, d), lambda i, s1, s2: (i, 0)),
                      pl.BlockSpec(memory_space=pl.ANY)],
            out_specs=pl.BlockSpec(memory_space=pl.ANY),
            scratch_shapes=[pltpu.VMEM((ROW_BLOCK, d // 2), jnp.uint32), pltpu.SemaphoreType.DMA]),
        out_shape=jax.ShapeDtypeStruct(xs0.shape, jnp.uint32),
        input_output_aliases={3: 0},
        compiler_params=_params("arbitrary"),
        name="moe_scatter",
    )(slot1, slot2, h, xs0)


def _row_copy(src_hbm, dst, src_row, r, sem):
    return pltpu.make_async_copy(src_hbm.at[pl.ds(src_row, 1), :], dst.at[pl.ds(r, 1), :], sem)


def _row_gather(src_hbm, gathers, base, n, sem):
    def start(r, c):
        for dst, idx_ref in gathers:
            _row_copy(src_hbm, dst, idx_ref[base + r], r, sem).start()
        return c

    def wait(r, c):
        for dst, _ in gathers:
            _row_copy(src_hbm, dst, 0, r, sem).wait()
        return c

    lax.fori_loop(0, n, start, 0)
    lax.fori_loop(0, n, wait, 0)


def _grouped_gated_body(te_ref, used_ref, a_ref, wg_ref, wu_ref, o_ref):
    @pl.when(pl.program_id(0) < used_ref[0])
    def _():
        words = a_ref[...]
        half = words.shape[1]
        hi = lax.bitcast_convert_type(words & jnp.uint32(0xFFFF0000), F32).astype(BF16)
        lo = lax.bitcast_convert_type(words << 16, F32).astype(BF16)
        g = (jnp.dot(hi, wg_ref[:half, :], preferred_element_type=F32)
             + jnp.dot(lo, wg_ref[half:, :], preferred_element_type=F32))
        u = (jnp.dot(hi, wu_ref[:half, :], preferred_element_type=F32)
             + jnp.dot(lo, wu_ref[half:, :], preferred_element_type=F32))
        o_ref[...] = (g * jax.nn.sigmoid(g) * u).astype(o_ref.dtype)

    @pl.when(pl.program_id(0) >= used_ref[0])
    def _():
        o_ref[...] = jnp.zeros_like(o_ref)


def _grouped_swiglu(xs, w, first, tile_expert, n_used, *, tm, tn):
    kk = w.shape[1]
    n_tiles = tile_expert.shape[0]
    s = n_tiles * tm
    f = w.shape[2] // 2
    nf = f // tn
    return pl.pallas_call(
        _grouped_gated_body,
        grid_spec=pltpu.PrefetchScalarGridSpec(
            num_scalar_prefetch=2,
            grid=(n_tiles, nf),
            in_specs=[pl.BlockSpec((tm, kk // 2), lambda i, j, te, used: (i, 0)),
                      pl.BlockSpec((None, kk, tn), lambda i, j, te, used: (first + te[i], 0, j)),
                      pl.BlockSpec((None, kk, tn), lambda i, j, te, used: (first + te[i], 0, nf + j))],
            out_specs=pl.BlockSpec((tm, tn), lambda i, j, te, used: (i, j))),
        out_shape=jax.ShapeDtypeStruct((s, f), BF16),
        compiler_params=_params("parallel", "parallel"),
        name="moe_grouped_swiglu",
    )(tile_expert, n_used, xs, w, w)


def _grouped_mm_body(te_ref, used_ref, a_ref, w_ref, o_ref):
    @pl.when(pl.program_id(0) < used_ref[0])
    def _():
        o_ref[...] = jnp.dot(a_ref[...], w_ref[...], preferred_element_type=F32).astype(o_ref.dtype)

    @pl.when(pl.program_id(0) >= used_ref[0])
    def _():
        o_ref[...] = jnp.zeros_like(o_ref)


def _grouped_matmul(a, w, first, tile_expert, n_used, *, tm, tn, out_dtype):
    s, kk = a.shape
    n = w.shape[2]
    return pl.pallas_call(
        _grouped_mm_body,
        grid_spec=pltpu.PrefetchScalarGridSpec(
            num_scalar_prefetch=2,
            grid=(s // tm, n // tn),
            in_specs=[pl.BlockSpec((tm, kk), lambda i, j, te, used: (i, 0)),
                      pl.BlockSpec((None, kk, tn), lambda i, j, te, used: (first + te[i], 0, j))],
            out_specs=pl.BlockSpec((tm, tn), lambda i, j, te, used: (i, j))),
        out_shape=jax.ShapeDtypeStruct((s, n), out_dtype),
        compiler_params=_params("parallel", "parallel"),
        name="moe_grouped_matmul",
    )(tile_expert, n_used, a, w)


def _post_moe_body(s1_ref, s2_ref, ys_hbm, x_ref, wts_ref, gp_ref, gate_ref, *rest, with_next, off):
    if with_next:
        gn_ref, sh_ref, sc_ref, xo_ref, h_ref, buf, sem = rest
    else:
        xo_ref, buf, sem = rest
    base = (pl.program_id(0) + off) * ROW_BLOCK
    _row_gather(ys_hbm, [(buf.at[0], s1_ref), (buf.at[1], s2_ref)], base, ROW_BLOCK, sem)
    wts = wts_ref[...]
    lane = lax.broadcasted_iota(jnp.int32, wts.shape, 1)
    w1 = jnp.sum(jnp.where(lane == 0, wts, 0.0), axis=1, keepdims=True)
    w2 = jnp.sum(jnp.where(lane == 1, wts, 0.0), axis=1, keepdims=True)
    y = w1 * buf[0] + w2 * buf[1]
    x_new = x_ref[...] + gate_ref[...] * _rms(y, gp_ref[...])
    xo_ref[...] = x_new
    if with_next:
        h = _rms(x_new, gn_ref[...]) * (1.0 + sc_ref[...]) + sh_ref[...]
        h_ref[...] = h.astype(h_ref.dtype)


def _postnorm_moe(ys, slot1, slot2, weights, x, g_post, gate, nxt=None, latent_only=False):
    m, d = x.shape
    nc = CTX_LEN // ROW_BLOCK
    off = nc if latent_only else 0
    n_ctx = 0 if latent_only else nc
    m_out = m - off * ROW_BLOCK
    row = lambda w: pl.BlockSpec((ROW_BLOCK, w), lambda i, s1, s2: (i + off, 0))
    vec = pl.BlockSpec((1, d), lambda i, s1, s2: (0, 0))
    modv = pl.BlockSpec((None, 1, d), lambda i, s1, s2: (jnp.where(i < n_ctx, 0, 1), 0, 0))
    out_row = lambda: pl.BlockSpec((ROW_BLOCK, d), lambda i, s1, s2: (i, 0))
    ins = [ys, x, weights, g_post.reshape(1, d), gate]
    in_specs = [pl.BlockSpec(memory_space=pl.ANY), row(d), row(LANES), vec, modv]
    out_shape = [jax.ShapeDtypeStruct((m_out, d), F32)]
    out_specs = [out_row()]
    if nxt is not None:
        ins += [nxt[0].reshape(1, d), nxt[1], nxt[2]]
        in_specs += [vec, modv, modv]
        out_shape.append(jax.ShapeDtypeStruct((m_out, d), BF16))
        out_specs.append(out_row())
    return pl.pallas_call(
        functools.partial(_post_moe_body, with_next=nxt is not None, off=off),
        grid_spec=pltpu.PrefetchScalarGridSpec(
            num_scalar_prefetch=2,
            grid=(m_out // ROW_BLOCK,),
            in_specs=in_specs,
            out_specs=out_specs,
            scratch_shapes=[pltpu.VMEM((2, ROW_BLOCK, d), F32), pltpu.SemaphoreType.DMA]),
        out_shape=out_shape,
        compiler_params=_params("arbitrary"),
        name="postnorm_moe",
    )(slot1, slot2, *ins)


def _s5_lag_body(b_ref, c_ref, o_ref):
    for g in range(b_ref.shape[0]):
        o_ref[g] = jnp.dot(b_ref[g], c_ref[g], preferred_element_type=F32,
                           precision=lax.Precision.HIGHEST)


def _s5_lag_kernels(b_lag, c_cat):
    g, r, n2 = b_lag.shape
    gb = S5_SLAB_GROUPS
    return pl.pallas_call(
        _s5_lag_body,
        grid=(g // gb,),
        in_specs=[pl.BlockSpec((gb, r, n2), lambda i: (i, 0, 0)),
                  pl.BlockSpec((gb, n2, S5_GROUP), lambda i: (i, 0, 0))],
        out_specs=pl.BlockSpec((gb, r, S5_GROUP), lambda i: (i, 0, 0)),
        out_shape=jax.ShapeDtypeStruct((g, r, S5_GROUP), F32),
        compiler_params=_params("parallel"),
        name="s5_lag_kernels",
    )(b_lag, c_cat)


def _chunk_tile(nc):
    return S5_CHUNK_TILE if nc % S5_CHUNK_TILE == 0 else nc


def _slab_rows(g, k):
    return pl.ds(k * LANES + g * S5_GROUP, S5_GROUP)


def _s5_in_body(u_ref, bc_ref, vre_ref, vim_ref, w_ref, *, tc):
    first = jnp.logical_and(pl.program_id(0) == 0, pl.program_id(1) == 0)

    @pl.when(first)
    def _():
        w_ref[...] = jnp.zeros_like(w_ref)

    @pl.when(pl.program_id(1) == 0)
    def _():
        for g in range(S5_SLAB_GROUPS):
            for k in range(S5_CHUNK):
                w_ref[g, _slab_rows(g, k), :] = bc_ref[g, k * S5_GROUP:(k + 1) * S5_GROUP, :]

    u = u_ref[...]
    for g in range(S5_SLAB_GROUPS):
        v = jnp.dot(u, w_ref[g], preferred_element_type=F32)
        rows = pl.ds(g, tc, stride=S5_SLAB_GROUPS)
        vre_ref[rows, :] = v[:, :LANES]
        vim_ref[rows, :] = v[:, LANES:]


def _s5_state_inputs(u, bc):
    ns, nc, sw = u.shape
    tc = _chunk_tile(nc)
    gb = S5_SLAB_GROUPS
    n4 = 4 * S5_STATE
    out = pl.BlockSpec((None, tc * gb, LANES), lambda s, m: (s, m, 0))
    return pl.pallas_call(
        functools.partial(_s5_in_body, tc=tc),
        grid=(ns, nc // tc),
        in_specs=[pl.BlockSpec((None, tc, sw), lambda s, m: (s, m, 0)),
                  pl.BlockSpec((gb, S5_CK, n4), lambda s, m: (s, 0, 0))],
        out_specs=[out, out],
        out_shape=[jax.ShapeDtypeStruct((ns, nc * gb, LANES), F32)] * 2,
        scratch_shapes=[pltpu.VMEM((gb, sw, n4), BF16)],
        compiler_params=_params("arbitrary", "arbitrary"),
        name="s5_state_inputs",
    )(u, bc)


def _s5_scan_body(vre_ref, vim_ref, a_ref, sre_ref, sim_ref, *, nc, nctx):
    gb = S5_SLAB_GROUPS
    a_re, a_im = a_ref[:, :LANES], a_ref[:, LANES:]
    fwd = lax.broadcasted_iota(jnp.int32, (gb, LANES), 1) < S5_STATE
    bwd = jnp.logical_not(fwd)

    def step(t, carry):
        s_re, s_im = carry
        t_b = jnp.where(t < nctx, nctx - 1 - t, nc - 1 - (t - nctx))
        rf = pl.ds(pl.multiple_of(t * gb, gb), gb)
        rb = pl.ds(pl.multiple_of(t_b * gb, gb), gb)
        pltpu.store(sre_ref.at[rf, :], s_re, mask=fwd)
        pltpu.store(sre_ref.at[rb, :], s_re, mask=bwd)
        pltpu.store(sim_ref.at[rf, :], s_im, mask=fwd)
        pltpu.store(sim_ref.at[rb, :], s_im, mask=bwd)
        v_re = jnp.where(fwd, vre_ref[rf, :], vre_ref[rb, :])
        v_im = jnp.where(fwd, vim_ref[rf, :], vim_ref[rb, :])
        return a_re * s_re - a_im * s_im + v_re, a_re * s_im + a_im * s_re + v_im

    zero = jnp.zeros((gb, LANES), F32)
    lax.fori_loop(0, nc, step, (zero, zero))


def _s5_scan(v_re, v_im, a_chunk):
    ns, rows, _ = v_re.shape
    gb = S5_SLAB_GROUPS
    nc = rows // gb
    blk = pl.BlockSpec((None, rows, LANES), lambda s: (s, 0, 0))
    return pl.pallas_call(
        functools.partial(_s5_scan_body, nc=nc, nctx=CTX_LEN // S5_CHUNK),
        grid=(ns,),
        in_specs=[blk, blk, pl.BlockSpec((None, gb, 2 * LANES), lambda s: (s, 0, 0))],
        out_specs=[blk, blk],
        out_shape=[jax.ShapeDtypeStruct((ns, rows, LANES), F32)] * 2,
        compiler_params=_params("parallel"),
        name="s5_scan",
    )(v_re, v_im, a_chunk)


def _s5_out_body(u_ref, sre_ref, sim_ref, t_ref, c_ref, y_ref, wt_ref, wc_ref, scat_ref, *, tc):
    gb = S5_SLAB_GROUPS
    n4 = 4 * S5_STATE

    @pl.when(pl.program_id(1) == 0)
    def _():
        row = lax.broadcasted_iota(jnp.int32, (S5_CK, S5_SLAB_W), 0)
        col = lax.broadcasted_iota(jnp.int32, (S5_CK, S5_SLAB_W), 1)
        spread = (row // S5_GROUP) * LANES + row % S5_GROUP
        for g in range(gb):
            place = (col == spread + g * S5_GROUP).astype(BF16)
            tg = jnp.dot(t_ref[g], place, preferred_element_type=F32).astype(BF16)
            for k in range(S5_CHUNK):
                wt_ref[_slab_rows(g, k), :] = tg[k * S5_GROUP:(k + 1) * S5_GROUP, :]
            wc_ref[g * n4:(g + 1) * n4, :] = jnp.dot(c_ref[g], place, preferred_element_type=F32).astype(BF16)

    for g in range(gb):
        rows = pl.ds(g, tc, stride=gb)
        scat_ref[:, g * n4:g * n4 + LANES] = sre_ref[rows, :].astype(BF16)
        scat_ref[:, g * n4 + LANES:(g + 1) * n4] = sim_ref[rows, :].astype(BF16)
    y = (jnp.dot(u_ref[...], wt_ref[...], preferred_element_type=F32)
         + jnp.dot(scat_ref[...], wc_ref[...], preferred_element_type=F32))
    y_ref[...] = jax.nn.gelu(y).astype(y_ref.dtype)


def _s5_outputs(u, s_re, s_im, toep, cc):
    ns, nc, sw = u.shape
    tc = _chunk_tile(nc)
    gb = S5_SLAB_GROUPS
    n4 = 4 * S5_STATE
    state = pl.BlockSpec((None, tc * gb, LANES), lambda i, m: (i, m, 0))
    return pl.pallas_call(
        functools.partial(_s5_out_body, tc=tc),
        grid=(ns, nc // tc),
        in_specs=[pl.BlockSpec((None, tc, sw), lambda i, m: (i, m, 0)), state, state,
                  pl.BlockSpec((gb, S5_CK, S5_CK), lambda i, m: (i, 0, 0)),
                  pl.BlockSpec((gb, n4, S5_CK), lambda i, m: (i, 0, 0))],
        out_specs=pl.BlockSpec((None, tc, sw), lambda i, m: (i, m, 0)),
        out_shape=jax.ShapeDtypeStruct((ns, nc, sw), BF16),
        scratch_shapes=[pltpu.VMEM((sw, sw), BF16), pltpu.VMEM((gb * n4, sw), BF16),
                        pltpu.VMEM((tc, gb * n4), BF16)],
        compiler_params=_params("parallel", "arbitrary"),
        name="s5_outputs",
    )(u, s_re, s_im, toep, cc)


def _s5_operators(a_re, a_im, log_dt, b_re, b_im, c_re, c_im, d_skip):
    L, G, N, C = S5_CHUNK, S5_GROUPS, S5_STATE, S5_GROUP
    a_re, a_im = a_re.astype(F32), a_im.astype(F32)
    b_re, b_im = b_re.astype(F32), b_im.astype(F32)
    c_re, c_im = c_re.astype(F32), c_im.astype(F32)
    dt = jnp.exp(log_dt.astype(F32))[..., None]
    steps = jnp.arange(L + 1, dtype=F32)[:, None, None, None]
    mag = jnp.exp((a_re * dt)[None] * steps)
    ang = (a_im * dt)[None] * steps
    pw_re, pw_im = mag * jnp.cos(ang), mag * jnp.sin(ang)
    p, q = pw_re[1] - 1.0, pw_im[1]
    den = a_re * a_re + a_im * a_im
    f_re, f_im = (p * a_re + q * a_im) / den, (q * a_re - p * a_im) / den
    bb_re = f_re[..., None] * b_re - f_im[..., None] * b_im
    bb_im = f_re[..., None] * b_im + f_im[..., None] * b_re

    def cmul(xr, xi, yr, yi):
        return xr * yr - xi * yi, xr * yi + xi * yr

    bl_re, bl_im = cmul(pw_re[:L, :, :, :, None], pw_im[:L, :, :, :, None], bb_re[None], bb_im[None])
    lag_rows = lambda z: jnp.transpose(z, (1, 2, 0, 4, 3)).reshape(2 * G, L * C, N)
    b_lag = jnp.concatenate([lag_rows(bl_re), -lag_rows(bl_im)], axis=-1)
    c_t = lambda z: jnp.swapaxes(z, -1, -2).reshape(2 * G, N, C)
    c_cat = jnp.concatenate([c_t(c_re), c_t(c_im)], axis=1)
    k_lag = _s5_lag_kernels(b_lag, c_cat).reshape(2, G, L, C, C)

    k_idx = jnp.arange(L)
    lag = k_idx[None, :] - k_idx[:, None]
    kf = jnp.where((lag >= 0)[None, :, :, None, None], k_lag[0][:, jnp.clip(lag, 0, L - 1)], 0.0)
    kr = jnp.where((lag <= 0)[None, :, :, None, None], k_lag[1][:, jnp.clip(-lag, 0, L - 1)], 0.0)
    eye_k = jnp.eye(L, dtype=F32)[None, :, :, None, None]
    d_diag = d_skip.astype(F32).reshape(G, 1, 1, C, 1) * jnp.eye(C, dtype=F32)[None, None, None]
    toep = kf + kr + eye_k * d_diag
    toep = jnp.transpose(toep, (0, 1, 3, 2, 4)).reshape(G, L * C, L * C)

    bf_re, bf_im = cmul(pw_re[L - 1 - k_idx, 0][..., None], pw_im[L - 1 - k_idx, 0][..., None],
                        bb_re[0][None], bb_im[0][None])
    bk_re, bk_im = cmul(pw_re[k_idx, 1][..., None], pw_im[k_idx, 1][..., None],
                        bb_re[1][None], bb_im[1][None])
    to_rows = lambda z: jnp.transpose(z, (1, 0, 3, 2)).reshape(G, L * C, N)
    bc = jnp.concatenate([to_rows(bf_re), to_rows(bk_re), to_rows(bf_im), to_rows(bk_im)], axis=-1)

    pw_g = lambda z, idx, d: jnp.transpose(z[idx, d], (1, 0, 2))[:, :, None, :]
    cf_re, cf_im = cmul(c_re[0][:, None], c_im[0][:, None], pw_g(pw_re, k_idx + 1, 0), pw_g(pw_im, k_idx + 1, 0))
    cb_re, cb_im = cmul(c_re[1][:, None], c_im[1][:, None], pw_g(pw_re, L - k_idx, 1), pw_g(pw_im, L - k_idx, 1))
    to_cols = lambda z: jnp.transpose(z, (0, 3, 1, 2)).reshape(G, N, L * C)
    cc = jnp.concatenate([to_cols(cf_re), to_cols(cb_re), -to_cols(cf_im), -to_cols(cb_im)], axis=1)

    a_chunk = jnp.concatenate([pw_re[L, 0], pw_re[L, 1], pw_im[L, 0], pw_im[L, 1]], axis=-1)
    a_chunk = a_chunk.reshape(S5_SLABS, S5_SLAB_GROUPS, 4 * N)
    return toep.astype(BF16), bc.astype(BF16), cc.astype(BF16), a_chunk


def _s5_mixer(h, ops):
    toep, bc, cc, a_chunk = ops
    m, d = h.shape
    nc = m // S5_CHUNK
    u = jnp.transpose(h.reshape(nc, S5_CHUNK, S5_SLABS, LANES), (2, 0, 1, 3)).reshape(S5_SLABS, nc, S5_SLAB_W)
    v_re, v_im = _s5_state_inputs(u, bc)
    s_re, s_im = _s5_scan(v_re, v_im, a_chunk)
    y = _s5_outputs(u, s_re, s_im, toep, cc)
    return jnp.transpose(y.reshape(S5_SLABS, nc, S5_CHUNK, LANES), (1, 2, 0, 3)).reshape(m, d)


def _na_bias_tables(rpb):
    var = jnp.arange(NA_KH)
    row_rel = jnp.arange(NA_KH)[None, :] - var[:, None] + (NA_KH - 1)
    q = jnp.arange(GRID_W)
    col_start = jnp.clip(q - NA_KW // 2, 0, GRID_W - NA_KW)
    kc = jnp.arange(GRID_W)
    inside = (kc[None, :] >= col_start[:, None]) & (kc[None, :] < col_start[:, None] + NA_KW)
    col_rel = jnp.clip(kc[None, :] - q[:, None] + (NA_KW - 1), 0, 2 * NA_KW - 2)
    t = rpb.astype(F32)[:, row_rel][:, :, :, col_rel]
    t = jnp.where(inside[None, None, None], t, NA_NEG)
    t = jnp.transpose(t, (1, 0, 3, 2, 4)).reshape(NA_KH, rpb.shape[0], GRID_W, NA_KH * GRID_W)

    def step_table(variants, offsets):
        rows = [jnp.pad(t[v], ((0, 0), (0, 0), (o * GRID_W, (NA_BAND - NA_KH - o) * GRID_W)),
                        constant_values=NA_NEG) for v, o in zip(variants, offsets)]
        return jnp.concatenate(rows, axis=1)

    r = NA_ROWS_PER_STEP
    half = NA_KH // 2
    first = step_table(range(r), [0] * r)
    inner = step_table([half] * r, range(r))
    last = step_table(range(half, half + r), [NA_BAND - NA_KH] * r)
    return jnp.stack([first, inner, last]) * LOG2_E


def _na_body(q_ref, k_ref, v_ref, bias_ref, o_ref, *, rows):
    i = pl.program_id(1)
    dn = (((1,), (1,)), ((), ()))
    nctx = CTX_LEN // (NA_ROWS_PER_STEP * GRID_W)
    band = NA_BAND * GRID_W
    heads = [pl.ds(hd * NA_HEAD_DIM, NA_HEAD_DIM) for hd in range(NA_HEADS_PER_STEP)]

    def ctx_scores(hd):
        return lax.dot_general(q_ref[:, hd], k_ref[0:CTX_LEN, hd], dn, preferred_element_type=F32)

    @pl.when(i < nctx)
    def _():
        for hd in heads:
            s_c = ctx_scores(hd)
            p = jnp.exp2(s_c - jnp.max(s_c, axis=-1, keepdims=True))
            o = jnp.dot(p.astype(BF16), v_ref[0:CTX_LEN, hd], preferred_element_type=F32)
            o_ref[:, hd] = (o / jnp.sum(p, axis=-1, keepdims=True)).astype(o_ref.dtype)

    @pl.when(i >= nctx)
    def _():
        r = (i - nctx) * NA_ROWS_PER_STEP
        rb = jnp.clip(r - NA_KH // 2, 0, rows - NA_BAND)
        kind = jnp.where(r == 0, 0, jnp.where(r == rows - NA_ROWS_PER_STEP, 2, 1))
        start = pl.multiple_of(CTX_LEN + rb * GRID_W, GRID_W)
        for n, hd in enumerate(heads):
            s_c = ctx_scores(hd)
            s_w = lax.dot_general(q_ref[:, hd], k_ref[pl.ds(start, band), hd], dn,
                                  preferred_element_type=F32) + bias_ref[kind, n]
            mx = jnp.maximum(jnp.max(s_w, axis=-1, keepdims=True), jnp.max(s_c, axis=-1, keepdims=True))
            p_w = jnp.exp2(s_w - mx)
            p_c = jnp.exp2(s_c - mx)
            den = jnp.sum(p_w, axis=-1, keepdims=True) + jnp.sum(p_c, axis=-1, keepdims=True)
            o = (jnp.dot(p_w.astype(BF16), v_ref[pl.ds(start, band), hd], preferred_element_type=F32)
                 + jnp.dot(p_c.astype(BF16), v_ref[0:CTX_LEN, hd], preferred_element_type=F32))
            o_ref[:, hd] = (o / den).astype(o_ref.dtype)


def _na_attention(qkv, bias):
    m = qkv.shape[0]
    rows = (m - CTX_LEN) // GRID_W
    assert rows % NA_ROWS_PER_STEP == 0 and rows >= NA_BAND + NA_ROWS_PER_STEP
    tq = NA_ROWS_PER_STEP * GRID_W
    hs = NA_HEADS_PER_STEP
    w = hs * NA_HEAD_DIM
    nh = NA_HEADS // hs
    return pl.pallas_call(
        functools.partial(_na_body, rows=rows),
        grid=(nh, m // tq),
        in_specs=[pl.BlockSpec((tq, w), lambda h, i: (i, h)),
                  pl.BlockSpec((m, w), lambda h, i: (0, nh + h)),
                  pl.BlockSpec((m, w), lambda h, i: (0, 2 * nh + h)),
                  pl.BlockSpec((3, hs, tq, NA_BAND * GRID_W), lambda h, i: (0, h, 0, 0))],
        out_specs=pl.BlockSpec((tq, w), lambda h, i: (i, h)),
        out_shape=jax.ShapeDtypeStruct((m, D_MODEL), BF16),
        compiler_params=_params("parallel", "arbitrary"),
        name="na_attention",
    )(qkv, qkv, qkv, bias)


def _tiles(m):
    for blocks in range(5, 0, -1):
        if m % (blocks * ROW_BLOCK) == 0:
            return blocks * ROW_BLOCK
    raise ValueError(m)


def kernel(x, c, ctx, c_ctx, ada_w, ada_b, norm_g, s5_a_re, s5_a_im, s5_log_dt, s5_b_re, s5_b_im,
           s5_c_re, s5_c_im, s5_d, s5_w_glu, na_w_qkv, na_w_o, na_rpb, ffn_w_gu, ffn_w_down,
           moe_w_router, moe_w_gu, moe_w_down):
    assert x.shape[0] == 1 and x.shape[2] == D_MODEL and ctx.shape[1] == CTX_LEN
    d = D_MODEL
    xc = jnp.concatenate([ctx[0], x[0]], axis=0).astype(F32)
    m = xc.shape[0]
    tm = _tiles(m)

    cond_t = jnp.stack([c_ctx, c[0]], axis=1).astype(F32)
    mods = _adaln(cond_t, ada_w.astype(F32), ada_b.astype(F32))
    mods = mods.reshape(DEPTH, 2, 6, 1, d)
    mod = lambda i, k: mods[i, :, k]

    w_glu, w_ffn_gu, w_ffn_down = s5_w_glu.astype(BF16), ffn_w_gu.astype(BF16), ffn_w_down.astype(BF16)
    w_qkv, w_o = na_w_qkv.astype(BF16), na_w_o.astype(BF16)
    n_odd = moe_w_gu.shape[0]
    w_moe_gu = moe_w_gu.astype(BF16).reshape(n_odd * N_EXPERTS, d, 2 * EXPERT_DIM)
    w_moe_down = moe_w_down.astype(BF16).reshape(n_odd * N_EXPERTS, EXPERT_DIM, d)

    h = _prenorm(xc, norm_g[0, 0], mod(0, 0), mod(0, 1))
    for i in range(DEPTH):
        j = i // 2
        last = i == DEPTH - 1
        nxt_f = (norm_g[i, 2], mod(i, 3), mod(i, 4))
        if i % 2 == 0:
            ops = _s5_operators(s5_a_re[j], s5_a_im[j], s5_log_dt[j], s5_b_re[j], s5_b_im[j],
                                s5_c_re[j], s5_c_im[j], s5_d[j])
            g = _s5_mixer(h, ops)
            y = _gated_matmul(g, w_glu, j, act="glu", tm=tm, tn=512, out_dtype=BF16)
            xc, h = _postnorm(y, xc, norm_g[i, 1], mod(i, 2), nxt_f)
            a = _gated_matmul(h, w_ffn_gu, j, act="swiglu", tm=tm, tn=512, out_dtype=BF16)
            y = _matmul(a, w_ffn_down, j, tm=tm, tn=1024, out_dtype=BF16)
        else:
            qkv = _matmul(h, w_qkv, j, tm=tm, tn=1024, out_dtype=BF16, scaled_cols=d, scale=NA_Q_SCALE)
            o = _na_attention(qkv, _na_bias_tables(na_rpb[j]))
            y = _matmul(o, w_o, j, tm=tm, tn=1024, out_dtype=BF16)
            xc, h, gates = _postnorm(y, xc, norm_g[i, 1], mod(i, 2), nxt_f, w_router=moe_w_router[j])
            tile_expert, n_used, slot1, slot2_scatter, slot2, weights, n_slots = _moe_routing(gates, tm)
            xs = _moe_scatter(h, slot1, slot2_scatter, n_slots)
            a = _grouped_swiglu(xs, w_moe_gu, j * N_EXPERTS, tile_expert, n_used, tm=tm, tn=512)
            ys = _grouped_matmul(a, w_moe_down, j * N_EXPERTS, tile_expert, n_used, tm=tm, tn=2048,
                                 out_dtype=F32)
            if last:
                out, = _postnorm_moe(ys, slot1, slot2, weights, xc, norm_g[i, 3], mod(i, 5), latent_only=True)
                return out[None].astype(x.dtype)
            nxt_m = (norm_g[i + 1, 0], mod(i + 1, 0), mod(i + 1, 1))
            xc, h = _postnorm_moe(ys, slot1, slot2, weights, xc, norm_g[i, 3], mod(i, 5), nxt_m)
            continue
        nxt_m = (norm_g[i + 1, 0], mod(i + 1, 0), mod(i + 1, 1))
        xc, h = _postnorm(y, xc, norm_g[i, 3], mod(i, 5), nxt_m)
```

```python
import functools

import jax
import jax.numpy as jnp
import numpy as np
from jax import lax
from jax.experimental import pallas as pl
from jax.experimental.pallas import tpu as pltpu

D_MODEL = 4096
DEPTH = 4
GRID_W = 64
CTX_LEN = 256
NORM_EPS = 1e-6
S5_GROUP = 16
S5_GROUPS = D_MODEL // S5_GROUP
S5_STATE = 64
NA_HEADS = 32
NA_HEAD_DIM = D_MODEL // NA_HEADS
NA_KH = 8
NA_KW = 16
N_EXPERTS = 8
EXPERT_DIM = 1024

LANES = 128
S5_CHUNK = 16
S5_CK = S5_CHUNK * S5_GROUP
S5_SLAB_GROUPS = LANES // S5_GROUP
S5_SLABS = S5_GROUPS // S5_SLAB_GROUPS
S5_SLAB_W = S5_CHUNK * LANES
S5_CHUNK_TILE = 520
NA_ROWS_PER_STEP = 4
NA_HEADS_PER_STEP = 4
NA_BAND = NA_KH + NA_ROWS_PER_STEP
NA_NEG = -1e30
LOG2_E = 1.4426950408889634
NA_Q_SCALE = NA_HEAD_DIM ** -0.5 * LOG2_E
ROW_BLOCK = 256
DMA_LOOP_UNROLL = 8
VMEM_LIMIT_V7X = 56 * 1024 * 1024

BF16 = jnp.bfloat16
F32 = jnp.float32


def _params(*sem):
    return pltpu.CompilerParams(dimension_semantics=sem, vmem_limit_bytes=VMEM_LIMIT_V7X)


def _adaln_body(cond_ref, w_ref, b_ref, o_ref):
    cond = cond_ref[...]
    s = cond * jax.nn.sigmoid(cond)
    w = w_ref[...]
    r0 = jnp.sum(w * s[:, 0:1], axis=0, keepdims=True)
    r1 = jnp.sum(w * s[:, 1:2], axis=0, keepdims=True)
    o_ref[...] = jnp.concatenate([r0, r1], axis=0) + b_ref[...]


def _adaln(cond_t, ada_w, ada_b, tn=512):
    depth, d, n = ada_w.shape
    return pl.pallas_call(
        _adaln_body,
        grid=(depth, n // tn),
        in_specs=[
            pl.BlockSpec((d, 2), lambda l, j: (0, 0)),
            pl.BlockSpec((None, d, tn), lambda l, j: (l, 0, j)),
            pl.BlockSpec((None, 1, tn), lambda l, j: (l, 0, j)),
        ],
        out_specs=pl.BlockSpec((None, 2, tn), lambda l, j: (l, 0, j)),
        out_shape=jax.ShapeDtypeStruct((depth, 2, n), F32),
        compiler_params=_params("parallel", "parallel"),
        name="adaln",
    )(cond_t, ada_w, ada_b.reshape(depth, 1, n))


def _rms(x, g):
    return x * lax.rsqrt(jnp.mean(x * x, axis=-1, keepdims=True) + NORM_EPS) * g


def _route(h, wr_ref, gates_ref):
    logits = jnp.dot(h, wr_ref[...], preferred_element_type=F32, precision=lax.Precision.HIGHEST)
    lane = lax.broadcasted_iota(jnp.int32, logits.shape, 1).astype(F32)
    n_lanes = float(logits.shape[1])
    logits = jnp.where(lane < N_EXPERTS, logits, -jnp.inf)
    m1 = jnp.max(logits, axis=-1, keepdims=True)
    i1 = jnp.min(jnp.where(logits == m1, lane, n_lanes), axis=-1, keepdims=True)
    rest = jnp.where(lane == i1, -jnp.inf, logits)
    m2 = jnp.max(rest, axis=-1, keepdims=True)
    i2 = jnp.min(jnp.where(rest == m2, lane, n_lanes), axis=-1, keepdims=True)
    e2 = jnp.exp(m2 - m1)
    w1 = 1.0 / (1.0 + e2)
    w2 = e2 / (1.0 + e2)
    gates_ref[...] = jnp.where(lane == i1, w1, 0.0) + jnp.where(lane == i2, w2, 0.0)


def _pre_body(x_ref, g_ref, sh_ref, sc_ref, h_ref):
    h = _rms(x_ref[...], g_ref[...]) * (1.0 + sc_ref[...]) + sh_ref[...]
    h_ref[...] = h.astype(h_ref.dtype)


def _post_body(*refs, with_next, with_router):
    y_ref, x_ref, gp_ref, gate_ref = refs[:4]
    x_new = x_ref[...] + gate_ref[...] * _rms(y_ref[...].astype(F32), gp_ref[...])
    if not with_next:
        refs[4][...] = x_new
        return
    gn_ref, sh_ref, sc_ref = refs[4:7]
    rest = refs[7:]
    if with_router:
        wr_ref, rest = rest[0], rest[1:]
    xo_ref, h_ref = rest[:2]
    xo_ref[...] = x_new
    h = _rms(x_new, gn_ref[...]) * (1.0 + sc_ref[...]) + sh_ref[...]
    h_ref[...] = h.astype(h_ref.dtype)
    if with_router:
        _route(h, wr_ref, rest[2])


def _row_spec(d, off=0):
    return pl.BlockSpec((ROW_BLOCK, d), lambda i: (i + off, 0))


def _vec_spec(d):
    return pl.BlockSpec((1, d), lambda i: (0, 0))


def _mod_spec(d, n_ctx_blocks):
    return pl.BlockSpec((None, 1, d), lambda i: (jnp.where(i < n_ctx_blocks, 0, 1), 0, 0))


def _prenorm(x, g, shift, scale):
    m, d = x.shape
    nc = CTX_LEN // ROW_BLOCK
    return pl.pallas_call(
        _pre_body,
        grid=(m // ROW_BLOCK,),
        in_specs=[_row_spec(d), _vec_spec(d), _mod_spec(d, nc), _mod_spec(d, nc)],
        out_specs=_row_spec(d),
        out_shape=jax.ShapeDtypeStruct((m, d), BF16),
        compiler_params=_params("parallel"),
        name="prenorm",
    )(x, g.reshape(1, d), shift, scale)


def _postnorm(y, x, g_post, gate, nxt=None, w_router=None, latent_only=False):
    m, d = x.shape
    nc = CTX_LEN // ROW_BLOCK
    off = nc if latent_only else 0
    n_ctx = 0 if latent_only else nc
    m_out = m - off * ROW_BLOCK
    ins = [y, x, g_post.reshape(1, d), gate]
    in_specs = [_row_spec(d, off), _row_spec(d, off), _vec_spec(d), _mod_spec(d, n_ctx)]
    out_shape = [jax.ShapeDtypeStruct((m_out, d), F32)]
    out_specs = [_row_spec(d)]
    if nxt is not None:
        ins += [nxt[0].reshape(1, d), nxt[1], nxt[2]]
        in_specs += [_vec_spec(d), _mod_spec(d, n_ctx), _mod_spec(d, n_ctx)]
        out_shape.append(jax.ShapeDtypeStruct((m_out, d), BF16))
        out_specs.append(_row_spec(d))
        if w_router is not None:
            wr = jnp.zeros((d, LANES), F32).at[:, :N_EXPERTS].set(w_router)
            ins.append(wr)
            in_specs.append(pl.BlockSpec((d, LANES), lambda i: (0, 0)))
            out_shape.append(jax.ShapeDtypeStruct((m_out, LANES), F32))
            out_specs.append(pl.BlockSpec((ROW_BLOCK, LANES), lambda i: (i, 0)))
    out = pl.pallas_call(
        functools.partial(_post_body, with_next=nxt is not None, with_router=w_router is not None),
        grid=(m_out // ROW_BLOCK,),
        in_specs=in_specs,
        out_specs=out_specs,
        out_shape=out_shape,
        compiler_params=_params("parallel"),
        name="postnorm",
    )(*ins)
    return out


def _mm_body(a_ref, w_ref, o_ref, *, scaled_tiles, scale):
    acc = jnp.dot(a_ref[...], w_ref[...], preferred_element_type=F32)
    if scaled_tiles:
        acc = acc * jnp.where(pl.program_id(1) < scaled_tiles, scale, 1.0)
    o_ref[...] = acc.astype(o_ref.dtype)


def _matmul(a, w, layer, *, tm, tn, out_dtype, scaled_cols=0, scale=1.0):
    m, kk = a.shape
    n = w.shape[2]
    assert scaled_cols % tn == 0
    return pl.pallas_call(
        functools.partial(_mm_body, scaled_tiles=scaled_cols // tn, scale=scale),
        grid=(m // tm, n // tn),
        in_specs=[pl.BlockSpec((tm, kk), lambda i, j: (i, 0)),
                  pl.BlockSpec((None, kk, tn), lambda i, j: (layer, 0, j))],
        out_specs=pl.BlockSpec((tm, tn), lambda i, j: (i, j)),
        out_shape=jax.ShapeDtypeStruct((m, n), out_dtype),
        compiler_params=_params("parallel", "parallel"),
        name="matmul",
    )(a, w)


def _gated_body(a_ref, wg_ref, wu_ref, o_ref, *, act):
    a = a_ref[...]
    g = jnp.dot(a, wg_ref[...], preferred_element_type=F32)
    u = jnp.dot(a, wu_ref[...], preferred_element_type=F32)
    if act == "swiglu":
        r = g * jax.nn.sigmoid(g) * u
    else:
        r = g * jax.nn.sigmoid(u)
    o_ref[...] = r.astype(o_ref.dtype)


def _gated_matmul(a, w, layer, *, act, tm, tn, out_dtype):
    m, kk = a.shape
    f = w.shape[2] // 2
    nf = f // tn
    return pl.pallas_call(
        functools.partial(_gated_body, act=act),
        grid=(m // tm, nf),
        in_specs=[pl.BlockSpec((tm, kk), lambda i, j: (i, 0)),
                  pl.BlockSpec((None, kk, tn), lambda i, j: (layer, 0, j)),
                  pl.BlockSpec((None, kk, tn), lambda i, j: (layer, 0, nf + j))],
        out_specs=pl.BlockSpec((tm, tn), lambda i, j: (i, j)),
        out_shape=jax.ShapeDtypeStruct((m, f), out_dtype),
        compiler_params=_params("parallel", "parallel"),
        name="gated_matmul",
    )(a, w, w)


def _moe_routing(gates, tm):
    m = gates.shape[0]
    g8 = gates[:, :N_EXPERTS]
    sel = g8 > 0.0
    pos = jnp.cumsum(sel.astype(jnp.int32), axis=0) - 1
    tiles_e = (pos[-1] + tm) // tm
    tile_end = jnp.cumsum(tiles_e)
    slot = ((tile_end - tiles_e) * tm)[None, :] + pos
    n_tiles = (2 * m + tm - 1) // tm + N_EXPERTS
    n_slots = n_tiles * tm
    tile_expert = jnp.minimum(jnp.searchsorted(tile_end, jnp.arange(n_tiles, dtype=jnp.int32), side="right"),
                              N_EXPERTS - 1).astype(jnp.int32)
    n_used = tile_end[-1:].astype(jnp.int32)
    e1 = jnp.argmax(g8, axis=1)
    w1 = jnp.max(g8, axis=1)
    rest = jnp.where(jnp.arange(N_EXPERTS)[None, :] == e1[:, None], -1.0, g8)
    e2 = jnp.argmax(rest, axis=1)
    w2 = jnp.maximum(jnp.max(rest, axis=1), 0.0)
    slot1 = jnp.take_along_axis(slot, e1[:, None], axis=1)[:, 0].astype(jnp.int32)
    slot2 = jnp.take_along_axis(slot, e2[:, None], axis=1)[:, 0].astype(jnp.int32)
    two = w2 > 0.0
    weights = jnp.zeros((m, LANES), F32).at[:, 0].set(w1).at[:, 1].set(w2)
    return (tile_expert, n_used, slot1, jnp.where(two, slot2, n_slots), jnp.where(two, slot2, slot1), weights,
            n_slots)


def _scatter_body(s1_ref, s2_ref, h_ref, xs_in, xs_out, buf, sem):
    del xs_in
    base = pl.program_id(0) * ROW_BLOCK
    bits = lax.bitcast_convert_type(h_ref[...].astype(F32), jnp.uint32)
    half = bits.shape[1] // 2
    buf[...] = bits[:, :half] | (bits[:, half:] >> 16)

    def copies(r, row1, row2):
        return (pltpu.make_async_copy(buf.at[pl.ds(r, 1), :], xs_out.at[pl.ds(row1, 1), :], sem),
                pltpu.make_async_copy(buf.at[pl.ds(r, 1), :], xs_out.at[pl.ds(row2, 1), :], sem))

    def start(r, c):
        for cp in copies(r, s1_ref[base + r], s2_ref[base + r]):
            cp.start()
        return c

    def wait(r, c):
        for cp in copies(r, 0, 0):
            cp.wait()
        return c

    lax.fori_loop(0, ROW_BLOCK, start, 0, unroll=DMA_LOOP_UNROLL)
    lax.fori_loop(0, ROW_BLOCK, wait, 0, unroll=DMA_LOOP_UNROLL)


def _moe_scatter(h, slot1, slot2, n_slots):
    m, d = h.shape
    spare = 8
    xs0 = jnp.zeros((n_slots + spare, d // 2), jnp.uint32)
    return pl.pallas_call(
        _scatter_body,
        grid_spec=pltpu.PrefetchScalarGridSpec(
            num_scalar_prefetch=2,
            grid=(m // ROW_BLOCK,),
            in_specs=[pl.BlockSpec((ROW_BLOCK, d), lambda i, s1, s2: (i, 0)),
                      pl.BlockSpec(memory_space=pl.ANY)],
            out_specs=pl.BlockSpec(memory_space=pl.ANY),
            scratch_shapes=[pltpu.VMEM((ROW_BLOCK, d // 2), jnp.uint32), pltpu.SemaphoreType.DMA]),
        out_shape=jax.ShapeDtypeStruct(xs0.shape, jnp.uint32),
        input_output_aliases={3: 0},
        compiler_params=_params("arbitrary"),
        name="moe_scatter",
    )(slot1, slot2, h, xs0)


def _row_copy(src_hbm, dst, src_row, r, sem):
    return pltpu.make_async_copy(src_hbm.at[pl.ds(src_row, 1), :], dst.at[pl.ds(r, 1), :], sem)


def _row_gather(src_hbm, gathers, base, n, sem):
    def start(r, c):
        for dst, idx_ref in gathers:
            _row_copy(src_hbm, dst, idx_ref[base + r], r, sem).start()
        return c

    def wait(r, c):
        for dst, _ in gathers:
            _row_copy(src_hbm, dst, 0, r, sem).wait()
        return c

    lax.fori_loop(0, n, start, 0, unroll=DMA_LOOP_UNROLL)
    lax.fori_loop(0, n, wait, 0, unroll=DMA_LOOP_UNROLL)


def _grouped_gated_body(te_ref, used_ref, a_ref, wg_ref, wu_ref, o_ref):
    @pl.when(pl.program_id(0) < used_ref[0])
    def _():
        words = a_ref[...]
        half = words.shape[1]
        hi = lax.bitcast_convert_type(words & jnp.uint32(0xFFFF0000), F32).astype(BF16)
        lo = lax.bitcast_convert_type(words << 16, F32).astype(BF16)
        g = (jnp.dot(hi, wg_ref[:half, :], preferred_element_type=F32)
             + jnp.dot(lo, wg_ref[half:, :], preferred_element_type=F32))
        u = (jnp.dot(hi, wu_ref[:half, :], preferred_element_type=F32)
             + jnp.dot(lo, wu_ref[half:, :], preferred_element_type=F32))
        o_ref[...] = (g * jax.nn.sigmoid(g) * u).astype(o_ref.dtype)

    @pl.when(pl.program_id(0) >= used_ref[0])
    def _():
        o_ref[...] = jnp.zeros_like(o_ref)


def _grouped_swiglu(xs, w, first, tile_expert, n_used, *, tm, tn):
    kk = w.shape[1]
    n_tiles = tile_expert.shape[0]
    s = n_tiles * tm
    f = w.shape[2] // 2
    nf = f // tn
    return pl.pallas_call(
        _grouped_gated_body,
        grid_spec=pltpu.PrefetchScalarGridSpec(
            num_scalar_prefetch=2,
            grid=(n_tiles, nf),
            in_specs=[pl.BlockSpec((tm, kk // 2), lambda i, j, te, used: (i, 0)),
                      pl.BlockSpec((None, kk, tn), lambda i, j, te, used: (first + te[i], 0, j)),
                      pl.BlockSpec((None, kk, tn), lambda i, j, te, used: (first + te[i], 0, nf + j))],
            out_specs=pl.BlockSpec((tm, tn), lambda i, j, te, used: (i, j))),
        out_shape=jax.ShapeDtypeStruct((s, f), BF16),
        compiler_params=_params("parallel", "parallel"),
        name="moe_grouped_swiglu",
    )(tile_expert, n_used, xs, w, w)


def _grouped_mm_body(te_ref, used_ref, a_ref, w_ref, o_ref):
    @pl.when(pl.program_id(0) < used_ref[0])
    def _():
        o_ref[...] = jnp.dot(a_ref[...], w_ref[...], preferred_element_type=F32).astype(o_ref.dtype)

    @pl.when(pl.program_id(0) >= used_ref[0])
    def _():
        o_ref[...] = jnp.zeros_like(o_ref)


def _grouped_matmul(a, w, first, tile_expert, n_used, *, tm, tn, out_dtype):
    s, kk = a.shape
    n = w.shape[2]
    return pl.pallas_call(
        _grouped_mm_body,
        grid_spec=pltpu.PrefetchScalarGridSpec(
            num_scalar_prefetch=2,
            grid=(s // tm, n // tn),
            in_specs=[pl.BlockSpec((tm, kk), lambda i, j, te, used: (i, 0)),
                      pl.BlockSpec((None, kk, tn), lambda i, j, te, used: (first + te[i], 0, j))],
            out_specs=pl.BlockSpec((tm, tn), lambda i, j, te, used: (i, j))),
        out_shape=jax.ShapeDtypeStruct((s, n), out_dtype),
        compiler_params=_params("parallel", "parallel"),
        name="moe_grouped_matmul",
    )(tile_expert, n_used, a, w)


def _post_moe_body(s1_ref, s2_ref, ys_hbm, x_ref, wts_ref, gp_ref, gate_ref, *rest, with_next, off):
    if with_next:
        gn_ref, sh_ref, sc_ref, xo_ref, h_ref, buf, sem = rest
    else:
        xo_ref, buf, sem = rest
    base = (pl.program_id(0) + off) * ROW_BLOCK
    _row_gather(ys_hbm, [(buf.at[0], s1_ref), (buf.at[1], s2_ref)], base, ROW_BLOCK, sem)
    wts = wts_ref[...]
    lane = lax.broadcasted_iota(jnp.int32, wts.shape, 1)
    w1 = jnp.sum(jnp.where(lane == 0, wts, 0.0), axis=1, keepdims=True)
    w2 = jnp.sum(jnp.where(lane == 1, wts, 0.0), axis=1, keepdims=True)
    y = w1 * buf[0] + w2 * buf[1]
    x_new = x_ref[...] + gate_ref[...] * _rms(y, gp_ref[...])
    xo_ref[...] = x_new
    if with_next:
        h = _rms(x_new, gn_ref[...]) * (1.0 + sc_ref[...]) + sh_ref[...]
        h_ref[...] = h.astype(h_ref.dtype)


def _postnorm_moe(ys, slot1, slot2, weights, x, g_post, gate, nxt=None, latent_only=False):
    m, d = x.shape
    nc = CTX_LEN // ROW_BLOCK
    off = nc if latent_only else 0
    n_ctx = 0 if latent_only else nc
    m_out = m - off * ROW_BLOCK
    row = lambda w: pl.BlockSpec((ROW_BLOCK, w), lambda i, s1, s2: (i + off, 0))
    vec = pl.BlockSpec((1, d), lambda i, s1, s2: (0, 0))
    modv = pl.BlockSpec((None, 1, d), lambda i, s1, s2: (jnp.where(i < n_ctx, 0, 1), 0, 0))
    out_row = lambda: pl.BlockSpec((ROW_BLOCK, d), lambda i, s1, s2: (i, 0))
    ins = [ys, x, weights, g_post.reshape(1, d), gate]
    in_specs = [pl.BlockSpec(memory_space=pl.ANY), row(d), row(LANES), vec, modv]
    out_shape = [jax.ShapeDtypeStruct((m_out, d), F32)]
    out_specs = [out_row()]
    if nxt is not None:
        ins += [nxt[0].reshape(1, d), nxt[1], nxt[2]]
        in_specs += [vec, modv, modv]
        out_shape.append(jax.ShapeDtypeStruct((m_out, d), BF16))
        out_specs.append(out_row())
    return pl.pallas_call(
        functools.partial(_post_moe_body, with_next=nxt is not None, off=off),
        grid_spec=pltpu.PrefetchScalarGridSpec(
            num_scalar_prefetch=2,
            grid=(m_out // ROW_BLOCK,),
            in_specs=in_specs,
            out_specs=out_specs,
            scratch_shapes=[pltpu.VMEM((2, ROW_BLOCK, d), F32), pltpu.SemaphoreType.DMA]),
        out_shape=out_shape,
        compiler_params=_params("arbitrary"),
        name="postnorm_moe",
    )(slot1, slot2, *ins)


def _s5_lag_body(b_ref, c_ref, o_ref):
    for g in range(b_ref.shape[0]):
        o_ref[g] = jnp.dot(b_ref[g], c_ref[g], preferred_element_type=F32,
                           precision=lax.Precision.HIGHEST)


def _s5_lag_kernels(b_lag, c_cat):
    g, r, n2 = b_lag.shape
    gb = S5_SLAB_GROUPS
    return pl.pallas_call(
        _s5_lag_body,
        grid=(g // gb,),
        in_specs=[pl.BlockSpec((gb, r, n2), lambda i: (i, 0, 0)),
                  pl.BlockSpec((gb, n2, S5_GROUP), lambda i: (i, 0, 0))],
        out_specs=pl.BlockSpec((gb, r, S5_GROUP), lambda i: (i, 0, 0)),
        out_shape=jax.ShapeDtypeStruct((g, r, S5_GROUP), F32),
        compiler_params=_params("parallel"),
        name="s5_lag_kernels",
    )(b_lag, c_cat)


def _chunk_tile(nc):
    return S5_CHUNK_TILE if nc % S5_CHUNK_TILE == 0 else nc


def _slab_rows(g, k):
    return pl.ds(k * LANES + g * S5_GROUP, S5_GROUP)


def _s5_in_body(u_ref, bc_ref, vre_ref, vim_ref, w_ref, *, tc):
    first = jnp.logical_and(pl.program_id(0) == 0, pl.program_id(1) == 0)

    @pl.when(first)
    def _():
        w_ref[...] = jnp.zeros_like(w_ref)

    @pl.when(pl.program_id(1) == 0)
    def _():
        for g in range(S5_SLAB_GROUPS):
            for k in range(S5_CHUNK):
                w_ref[g, _slab_rows(g, k), :] = bc_ref[g, k * S5_GROUP:(k + 1) * S5_GROUP, :]

    u = u_ref[...]
    for g in range(S5_SLAB_GROUPS):
        v = jnp.dot(u, w_ref[g], preferred_element_type=F32)
        rows = pl.ds(g, tc, stride=S5_SLAB_GROUPS)
        vre_ref[rows, :] = v[:, :LANES]
        vim_ref[rows, :] = v[:, LANES:]


def _s5_state_inputs(u, bc):
    ns, nc, sw = u.shape
    tc = _chunk_tile(nc)
    gb = S5_SLAB_GROUPS
    n4 = 4 * S5_STATE
    out = pl.BlockSpec((None, tc * gb, LANES), lambda s, m: (s, m, 0))
    return pl.pallas_call(
        functools.partial(_s5_in_body, tc=tc),
        grid=(ns, nc // tc),
        in_specs=[pl.BlockSpec((None, tc, sw), lambda s, m: (s, m, 0)),
                  pl.BlockSpec((gb, S5_CK, n4), lambda s, m: (s, 0, 0))],
        out_specs=[out, out],
        out_shape=[jax.ShapeDtypeStruct((ns, nc * gb, LANES), F32)] * 2,
        scratch_shapes=[pltpu.VMEM((gb, sw, n4), BF16)],
        compiler_params=_params("arbitrary", "arbitrary"),
        name="s5_state_inputs",
    )(u, bc)


def _s5_scan_body(vre_ref, vim_ref, a_ref, sre_ref, sim_ref, *, nc, nctx):
    gb = S5_SLAB_GROUPS
    a_re, a_im = a_ref[:, :LANES], a_ref[:, LANES:]
    fwd = lax.broadcasted_iota(jnp.int32, (gb, LANES), 1) < S5_STATE
    bwd = jnp.logical_not(fwd)

    def step(t, carry):
        s_re, s_im = carry
        t_b = jnp.where(t < nctx, nctx - 1 - t, nc - 1 - (t - nctx))
        rf = pl.ds(pl.multiple_of(t * gb, gb), gb)
        rb = pl.ds(pl.multiple_of(t_b * gb, gb), gb)
        pltpu.store(sre_ref.at[rf, :], s_re, mask=fwd)
        pltpu.store(sre_ref.at[rb, :], s_re, mask=bwd)
        pltpu.store(sim_ref.at[rf, :], s_im, mask=fwd)
        pltpu.store(sim_ref.at[rb, :], s_im, mask=bwd)
        v_re = jnp.where(fwd, vre_ref[rf, :], vre_ref[rb, :])
        v_im = jnp.where(fwd, vim_ref[rf, :], vim_ref[rb, :])
        return a_re * s_re - a_im * s_im + v_re, a_re * s_im + a_im * s_re + v_im

    zero = jnp.zeros((gb, LANES), F32)
    lax.fori_loop(0, nc, step, (zero, zero))


def _s5_scan(v_re, v_im, a_chunk):
    ns, rows, _ = v_re.shape
    gb = S5_SLAB_GROUPS
    nc = rows // gb
    blk = pl.BlockSpec((None, rows, LANES), lambda s: (s, 0, 0))
    return pl.pallas_call(
        functools.partial(_s5_scan_body, nc=nc, nctx=CTX_LEN // S5_CHUNK),
        grid=(ns,),
        in_specs=[blk, blk, pl.BlockSpec((None, gb, 2 * LANES), lambda s: (s, 0, 0))],
        out_specs=[blk, blk],
        out_shape=[jax.ShapeDtypeStruct((ns, rows, LANES), F32)] * 2,
        compiler_params=_params("parallel"),
        name="s5_scan",
    )(v_re, v_im, a_chunk)


def _s5_out_body(u_ref, sre_ref, sim_ref, t_ref, c_ref, y_ref, wt_ref, wc_ref, scat_ref, *, tc):
    gb = S5_SLAB_GROUPS
    n4 = 4 * S5_STATE

    @pl.when(pl.program_id(1) == 0)
    def _():
        row = lax.broadcasted_iota(jnp.int32, (S5_CK, S5_SLAB_W), 0)
        col = lax.broadcasted_iota(jnp.int32, (S5_CK, S5_SLAB_W), 1)
        spread = (row // S5_GROUP) * LANES + row % S5_GROUP
        for g in range(gb):
            place = (col == spread + g * S5_GROUP).astype(BF16)
            tg = jnp.dot(t_ref[g], place, preferred_element_type=F32).astype(BF16)
            for k in range(S5_CHUNK):
                wt_ref[_slab_rows(g, k), :] = tg[k * S5_GROUP:(k + 1) * S5_GROUP, :]
            wc_ref[g * n4:(g + 1) * n4, :] = jnp.dot(c_ref[g], place, preferred_element_type=F32).astype(BF16)

    for g in range(gb):
        rows = pl.ds(g, tc, stride=gb)
        scat_ref[:, g * n4:g * n4 + LANES] = sre_ref[rows, :].astype(BF16)
        scat_ref[:, g * n4 + LANES:(g + 1) * n4] = sim_ref[rows, :].astype(BF16)
    y = (jnp.dot(u_ref[...], wt_ref[...], preferred_element_type=F32)
         + jnp.dot(scat_ref[...], wc_ref[...], preferred_element_type=F32))
    y_ref[...] = jax.nn.gelu(y).astype(y_ref.dtype)


def _s5_outputs(u, s_re, s_im, toep, cc):
    ns, nc, sw = u.shape
    tc = _chunk_tile(nc)
    gb = S5_SLAB_GROUPS
    n4 = 4 * S5_STATE
    state = pl.BlockSpec((None, tc * gb, LANES), lambda i, m: (i, m, 0))
    return pl.pallas_call(
        functools.partial(_s5_out_body, tc=tc),
        grid=(ns, nc // tc),
        in_specs=[pl.BlockSpec((None, tc, sw), lambda i, m: (i, m, 0)), state, state,
                  pl.BlockSpec((gb, S5_CK, S5_CK), lambda i, m: (i, 0, 0)),
                  pl.BlockSpec((gb, n4, S5_CK), lambda i, m: (i, 0, 0))],
        out_specs=pl.BlockSpec((None, tc, sw), lambda i, m: (i, m, 0)),
        out_shape=jax.ShapeDtypeStruct((ns, nc, sw), BF16),
        scratch_shapes=[pltpu.VMEM((sw, sw), BF16), pltpu.VMEM((gb * n4, sw), BF16),
                        pltpu.VMEM((tc, gb * n4), BF16)],
        compiler_params=_params("parallel", "arbitrary"),
        name="s5_outputs",
    )(u, s_re, s_im, toep, cc)


def _s5_operators(a_re, a_im, log_dt, b_re, b_im, c_re, c_im, d_skip):
    L, G, N, C = S5_CHUNK, S5_GROUPS, S5_STATE, S5_GROUP
    a_re, a_im = a_re.astype(F32), a_im.astype(F32)
    b_re, b_im = b_re.astype(F32), b_im.astype(F32)
    c_re, c_im = c_re.astype(F32), c_im.astype(F32)
    dt = jnp.exp(log_dt.astype(F32))[..., None]
    steps = jnp.arange(L + 1, dtype=F32)[:, None, None, None]
    mag = jnp.exp((a_re * dt)[None] * steps)
    ang = (a_im * dt)[None] * steps
    pw_re, pw_im = mag * jnp.cos(ang), mag * jnp.sin(ang)
    p, q = pw_re[1] - 1.0, pw_im[1]
    den = a_re * a_re + a_im * a_im
    f_re, f_im = (p * a_re + q * a_im) / den, (q * a_re - p * a_im) / den
    bb_re = f_re[..., None] * b_re - f_im[..., None] * b_im
    bb_im = f_re[..., None] * b_im + f_im[..., None] * b_re

    def cmul(xr, xi, yr, yi):
        return xr * yr - xi * yi, xr * yi + xi * yr

    bl_re, bl_im = cmul(pw_re[:L, :, :, :, None], pw_im[:L, :, :, :, None], bb_re[None], bb_im[None])
    lag_rows = lambda z: jnp.transpose(z, (1, 2, 0, 4, 3)).reshape(2 * G, L * C, N)
    b_lag = jnp.concatenate([lag_rows(bl_re), -lag_rows(bl_im)], axis=-1)
    c_t = lambda z: jnp.swapaxes(z, -1, -2).reshape(2 * G, N, C)
    c_cat = jnp.concatenate([c_t(c_re), c_t(c_im)], axis=1)
    k_lag = _s5_lag_kernels(b_lag, c_cat).reshape(2, G, L, C, C)

    k_idx = jnp.arange(L)
    lag = k_idx[None, :] - k_idx[:, None]
    kf = jnp.where((lag >= 0)[None, :, :, None, None], k_lag[0][:, jnp.clip(lag, 0, L - 1)], 0.0)
    kr = jnp.where((lag <= 0)[None, :, :, None, None], k_lag[1][:, jnp.clip(-lag, 0, L - 1)], 0.0)
    eye_k = jnp.eye(L, dtype=F32)[None, :, :, None, None]
    d_diag = d_skip.astype(F32).reshape(G, 1, 1, C, 1) * jnp.eye(C, dtype=F32)[None, None, None]
    toep = kf + kr + eye_k * d_diag
    toep = jnp.transpose(toep, (0, 1, 3, 2, 4)).reshape(G, L * C, L * C)

    bf_re, bf_im = cmul(pw_re[L - 1 - k_idx, 0][..., None], pw_im[L - 1 - k_idx, 0][..., None],
                        bb_re[0][None], bb_im[0][None])
    bk_re, bk_im = cmul(pw_re[k_idx, 1][..., None], pw_im[k_idx, 1][..., None],
                        bb_re[1][None], bb_im[1][None])
    to_rows = lambda z: jnp.transpose(z, (1, 0, 3, 2)).reshape(G, L * C, N)
    bc = jnp.concatenate([to_rows(bf_re), to_rows(bk_re), to_rows(bf_im), to_rows(bk_im)], axis=-1)

    pw_g = lambda z, idx, d: jnp.transpose(z[idx, d], (1, 0, 2))[:, :, None, :]
    cf_re, cf_im = cmul(c_re[0][:, None], c_im[0][:, None], pw_g(pw_re, k_idx + 1, 0), pw_g(pw_im, k_idx + 1, 0))
    cb_re, cb_im = cmul(c_re[1][:, None], c_im[1][:, None], pw_g(pw_re, L - k_idx, 1), pw_g(pw_im, L - k_idx, 1))
    to_cols = lambda z: jnp.transpose(z, (0, 3, 1, 2)).reshape(G, N, L * C)
    cc = jnp.concatenate([to_cols(cf_re), to_cols(cb_re), -to_cols(cf_im), -to_cols(cb_im)], axis=1)

    a_chunk = jnp.concatenate([pw_re[L, 0], pw_re[L, 1], pw_im[L, 0], pw_im[L, 1]], axis=-1)
    a_chunk = a_chunk.reshape(S5_SLABS, S5_SLAB_GROUPS, 4 * N)
    return toep.astype(BF16), bc.astype(BF16), cc.astype(BF16), a_chunk


def _s5_mixer(h, ops):
    toep, bc, cc, a_chunk = ops
    m, d = h.shape
    nc = m // S5_CHUNK
    u = jnp.transpose(h.reshape(nc, S5_CHUNK, S5_SLABS, LANES), (2, 0, 1, 3)).reshape(S5_SLABS, nc, S5_SLAB_W)
    v_re, v_im = _s5_state_inputs(u, bc)
    s_re, s_im = _s5_scan(v_re, v_im, a_chunk)
    y = _s5_outputs(u, s_re, s_im, toep, cc)
    return jnp.transpose(y.reshape(S5_SLABS, nc, S5_CHUNK, LANES), (1, 2, 0, 3)).reshape(m, d)


def _na_bias_index():
    r, half = NA_ROWS_PER_STEP, NA_KH // 2
    kinds = [(list(range(r)), [0] * r),
             ([half] * r, list(range(r))),
             (list(range(half, half + r)), [NA_BAND - NA_KH] * r)]
    variant = np.array([k[0] for k in kinds])[:, :, None, None, None]
    offset = np.array([k[1] for k in kinds])[:, :, None, None, None]
    qc = np.arange(GRID_W)[None, None, :, None, None]
    jb = np.arange(NA_BAND)[None, None, None, :, None]
    kc = np.arange(GRID_W)[None, None, None, None, :]
    j = jb - offset
    col_start = np.clip(qc - NA_KW // 2, 0, GRID_W - NA_KW)
    valid = (j >= 0) & (j < NA_KH) & (kc >= col_start) & (kc < col_start + NA_KW)
    row_rel = np.clip(j - variant + (NA_KH - 1), 0, 2 * NA_KH - 2)
    col_rel = np.clip(kc - qc + (NA_KW - 1), 0, 2 * NA_KW - 2)
    shape = (len(kinds), r * GRID_W, NA_BAND * GRID_W)
    idx = np.broadcast_to(row_rel * (2 * NA_KW - 1) + col_rel, valid.shape)
    return idx.reshape(shape).astype(np.int32), np.broadcast_to(valid, valid.shape).reshape(shape)


def _na_bias_tables(rpb):
    idx, valid = _na_bias_index()
    flat = rpb.astype(F32).reshape(rpb.shape[0], -1) * LOG2_E
    t = jnp.take(flat, jnp.asarray(idx.reshape(-1)), axis=1).reshape((rpb.shape[0],) + idx.shape)
    return jnp.where(jnp.asarray(valid)[None], t, NA_NEG)


def _na_body(q_ref, k_ref, v_ref, bias_ref, o_ref, *, rows):
    i = pl.program_id(1)
    dn = (((1,), (1,)), ((), ()))
    nctx = CTX_LEN // (NA_ROWS_PER_STEP * GRID_W)
    band = NA_BAND * GRID_W
    heads = [pl.ds(hd * NA_HEAD_DIM, NA_HEAD_DIM) for hd in range(NA_HEADS_PER_STEP)]

    def ctx_scores(hd):
        return lax.dot_general(q_ref[:, hd], k_ref[0:CTX_LEN, hd], dn, preferred_element_type=F32)

    @pl.when(i < nctx)
    def _():
        for hd in heads:
            s_c = ctx_scores(hd)
            p = jnp.exp2(s_c - jnp.max(s_c, axis=-1, keepdims=True))
            o = jnp.dot(p.astype(BF16), v_ref[0:CTX_LEN, hd], preferred_element_type=F32)
            o_ref[:, hd] = (o / jnp.sum(p, axis=-1, keepdims=True)).astype(o_ref.dtype)

    @pl.when(i >= nctx)
    def _():
        r = (i - nctx) * NA_ROWS_PER_STEP
        rb = jnp.clip(r - NA_KH // 2, 0, rows - NA_BAND)
        kind = jnp.where(r == 0, 0, jnp.where(r == rows - NA_ROWS_PER_STEP, 2, 1))
        start = pl.multiple_of(CTX_LEN + rb * GRID_W, GRID_W)
        for n, hd in enumerate(heads):
            s_c = ctx_scores(hd)
            s_w = lax.dot_general(q_ref[:, hd], k_ref[pl.ds(start, band), hd], dn,
                                  preferred_element_type=F32) + bias_ref[n, kind]
            mx = jnp.maximum(jnp.max(s_w, axis=-1, keepdims=True), jnp.max(s_c, axis=-1, keepdims=True))
            p_w = jnp.exp2(s_w - mx)
            p_c = jnp.exp2(s_c - mx)
            den = jnp.sum(p_w, axis=-1, keepdims=True) + jnp.sum(p_c, axis=-1, keepdims=True)
            o = (jnp.dot(p_w.astype(BF16), v_ref[pl.ds(start, band), hd], preferred_element_type=F32)
                 + jnp.dot(p_c.astype(BF16), v_ref[0:CTX_LEN, hd], preferred_element_type=F32))
            o_ref[:, hd] = (o / den).astype(o_ref.dtype)


def _na_attention(qkv, bias):
    m = qkv.shape[0]
    rows = (m - CTX_LEN) // GRID_W
    assert rows % NA_ROWS_PER_STEP == 0 and rows >= NA_BAND + NA_ROWS_PER_STEP
    tq = NA_ROWS_PER_STEP * GRID_W
    hs = NA_HEADS_PER_STEP
    w = hs * NA_HEAD_DIM
    nh = NA_HEADS // hs
    return pl.pallas_call(
        functools.partial(_na_body, rows=rows),
        grid=(nh, m // tq),
        in_specs=[pl.BlockSpec((tq, w), lambda h, i: (i, h)),
                  pl.BlockSpec((m, w), lambda h, i: (0, nh + h), pipeline_mode=pl.Buffered(1)),
                  pl.BlockSpec((m, w), lambda h, i: (0, 2 * nh + h), pipeline_mode=pl.Buffered(1)),
                  pl.BlockSpec((hs, 3, tq, NA_BAND * GRID_W), lambda h, i: (h, 0, 0, 0),
                               pipeline_mode=pl.Buffered(1))],
        out_specs=pl.BlockSpec((tq, w), lambda h, i: (i, h)),
        out_shape=jax.ShapeDtypeStruct((m, D_MODEL), BF16),
        compiler_params=_params("parallel", "arbitrary"),
        name="na_attention",
    )(qkv, qkv, qkv, bias)


def _tiles(m):
    for blocks in range(5, 0, -1):
        if m % (blocks * ROW_BLOCK) == 0:
            return blocks * ROW_BLOCK
    raise ValueError(m)


def kernel(x, c, ctx, c_ctx, ada_w, ada_b, norm_g, s5_a_re, s5_a_im, s5_log_dt, s5_b_re, s5_b_im,
           s5_c_re, s5_c_im, s5_d, s5_w_glu, na_w_qkv, na_w_o, na_rpb, ffn_w_gu, ffn_w_down,
           moe_w_router, moe_w_gu, moe_w_down):
    assert x.shape[0] == 1 and x.shape[2] == D_MODEL and ctx.shape[1] == CTX_LEN
    d = D_MODEL
    xc = jnp.concatenate([ctx[0], x[0]], axis=0).astype(F32)
    m = xc.shape[0]
    tm = _tiles(m)

    cond_t = jnp.stack([c_ctx, c[0]], axis=1).astype(F32)
    mods = _adaln(cond_t, ada_w.astype(F32), ada_b.astype(F32))
    mods = mods.reshape(DEPTH, 2, 6, 1, d)
    mod = lambda i, k: mods[i, :, k]

    w_glu, w_ffn_gu, w_ffn_down = s5_w_glu.astype(BF16), ffn_w_gu.astype(BF16), ffn_w_down.astype(BF16)
    w_qkv, w_o = na_w_qkv.astype(BF16), na_w_o.astype(BF16)
    n_odd = moe_w_gu.shape[0]
    w_moe_gu = moe_w_gu.astype(BF16).reshape(n_odd * N_EXPERTS, d, 2 * EXPERT_DIM)
    w_moe_down = moe_w_down.astype(BF16).reshape(n_odd * N_EXPERTS, EXPERT_DIM, d)

    h = _prenorm(xc, norm_g[0, 0], mod(0, 0), mod(0, 1))
    for i in range(DEPTH):
        j = i // 2
        last = i == DEPTH - 1
        nxt_f = (norm_g[i, 2], mod(i, 3), mod(i, 4))
        if i % 2 == 0:
            ops = _s5_operators(s5_a_re[j], s5_a_im[j], s5_log_dt[j], s5_b_re[j], s5_b_im[j],
                                s5_c_re[j], s5_c_im[j], s5_d[j])
            g = _s5_mixer(h, ops)
            y = _gated_matmul(g, w_glu, j, act="glu", tm=tm, tn=512, out_dtype=BF16)
            xc, h = _postnorm(y, xc, norm_g[i, 1], mod(i, 2), nxt_f)
            a = _gated_matmul(h, w_ffn_gu, j, act="swiglu", tm=tm, tn=512, out_dtype=BF16)
            y = _matmul(a, w_ffn_down, j, tm=tm, tn=1024, out_dtype=BF16)
        else:
            qkv = _matmul(h, w_qkv, j, tm=tm, tn=1024, out_dtype=BF16, scaled_cols=d, scale=NA_Q_SCALE)
            o = _na_attention(qkv, _na_bias_tables(na_rpb[j]))
            y = _matmul(o, w_o, j, tm=tm, tn=1024, out_dtype=BF16)
            xc, h, gates = _postnorm(y, xc, norm_g[i, 1], mod(i, 2), nxt_f, w_router=moe_w_router[j])
            tile_expert, n_used, slot1, slot2_scatter, slot2, weights, n_slots = _moe_routing(gates, tm)
            xs = _moe_scatter(h, slot1, slot2_scatter, n_slots)
            a = _grouped_swiglu(xs, w_moe_gu, j * N_EXPERTS, tile_expert, n_used, tm=tm, tn=512)
            ys = _grouped_matmul(a, w_moe_down, j * N_EXPERTS, tile_expert, n_used, tm=tm, tn=2048,
                                 out_dtype=F32)
            if last:
                out, = _postnorm_moe(ys, slot1, slot2, weights, xc, norm_g[i, 3], mod(i, 5), latent_only=True)
                return out[None].astype(x.dtype)
            nxt_m = (norm_g[i + 1, 0], mod(i + 1, 0), mod(i + 1, 1))
            xc, h = _postnorm_moe(ys, slot1, slot2, weights, xc, norm_g[i, 3], mod(i, 5), nxt_m)
            continue
        nxt_m = (norm_g[i + 1, 0], mod(i + 1, 0), mod(i + 1, 1))
        xc, h = _postnorm(y, xc, norm_g[i, 3], mod(i, 5), nxt_m)
```

```python
import functools

import jax
import jax.numpy as jnp
from jax import lax
from jax.experimental import pallas as pl
from jax.experimental.pallas import tpu as pltpu

D_MODEL = 4096
DEPTH = 4
GRID_W = 64
CTX_LEN = 256
NORM_EPS = 1e-6
S5_GROUP = 16
S5_GROUPS = D_MODEL // S5_GROUP
S5_STATE = 64
NA_HEADS = 32
NA_HEAD_DIM = D_MODEL // NA_HEADS
NA_KH = 8
NA_KW = 16
N_EXPERTS = 8
EXPERT_DIM = 1024

LANES = 128
S5_CHUNK = 16
S5_CK = S5_CHUNK * S5_GROUP
S5_SLAB_GROUPS = LANES // S5_GROUP
S5_SLABS = S5_GROUPS // S5_SLAB_GROUPS
S5_SLAB_W = S5_CHUNK * LANES
S5_CHUNK_TILE = 520
NA_ROWS_PER_STEP = 4
NA_HEADS_PER_STEP = 4
NA_BAND = NA_KH + NA_ROWS_PER_STEP
NA_NEG = -1e30
LOG2_E = 1.4426950408889634
NA_Q_SCALE = NA_HEAD_DIM ** -0.5 * LOG2_E
ROW_BLOCK = 256
DMA_LOOP_UNROLL = 8
VMEM_LIMIT_V7X = 56 * 1024 * 1024

BF16 = jnp.bfloat16
F32 = jnp.float32


def _params(*sem):
    return pltpu.CompilerParams(dimension_semantics=sem, vmem_limit_bytes=VMEM_LIMIT_V7X)


def _adaln_body(cond_ref, w_ref, b_ref, o_ref):
    cond = cond_ref[...]
    s = cond * jax.nn.sigmoid(cond)
    w = w_ref[...]
    r0 = jnp.sum(w * s[:, 0:1], axis=0, keepdims=True)
    r1 = jnp.sum(w * s[:, 1:2], axis=0, keepdims=True)
    o_ref[...] = jnp.concatenate([r0, r1], axis=0) + b_ref[...]


def _adaln(cond_t, ada_w, ada_b, tn=512):
    depth, d, n = ada_w.shape
    return pl.pallas_call(
        _adaln_body,
        grid=(depth, n // tn),
        in_specs=[
            pl.BlockSpec((d, 2), lambda l, j: (0, 0)),
            pl.BlockSpec((None, d, tn), lambda l, j: (l, 0, j)),
            pl.BlockSpec((None, 1, tn), lambda l, j: (l, 0, j)),
        ],
        out_specs=pl.BlockSpec((None, 2, tn), lambda l, j: (l, 0, j)),
        out_shape=jax.ShapeDtypeStruct((depth, 2, n), F32),
        compiler_params=_params("parallel", "parallel"),
        name="adaln",
    )(cond_t, ada_w, ada_b.reshape(depth, 1, n))


def _rms(x, g):
    return x * lax.rsqrt(jnp.mean(x * x, axis=-1, keepdims=True) + NORM_EPS) * g


def _route(h, wr_ref, gates_ref):
    logits = jnp.dot(h, wr_ref[...], preferred_element_type=F32, precision=lax.Precision.HIGHEST)
    lane = lax.broadcasted_iota(jnp.int32, logits.shape, 1).astype(F32)
    n_lanes = float(logits.shape[1])
    logits = jnp.where(lane < N_EXPERTS, logits, -jnp.inf)
    m1 = jnp.max(logits, axis=-1, keepdims=True)
    i1 = jnp.min(jnp.where(logits == m1, lane, n_lanes), axis=-1, keepdims=True)
    rest = jnp.where(lane == i1, -jnp.inf, logits)
    m2 = jnp.max(rest, axis=-1, keepdims=True)
    i2 = jnp.min(jnp.where(rest == m2, lane, n_lanes), axis=-1, keepdims=True)
    e2 = jnp.exp(m2 - m1)
    w1 = 1.0 / (1.0 + e2)
    w2 = e2 / (1.0 + e2)
    gates_ref[...] = jnp.where(lane == i1, w1, 0.0) + jnp.where(lane == i2, w2, 0.0)


def _pre_body(x_ref, g_ref, sh_ref, sc_ref, h_ref):
    h = _rms(x_ref[...], g_ref[...]) * (1.0 + sc_ref[...]) + sh_ref[...]
    h_ref[...] = h.astype(h_ref.dtype)


def _post_body(*refs, with_next, with_router):
    y_ref, x_ref, gp_ref, gate_ref = refs[:4]
    x_new = x_ref[...] + gate_ref[...] * _rms(y_ref[...].astype(F32), gp_ref[...])
    if not with_next:
        refs[4][...] = x_new
        return
    gn_ref, sh_ref, sc_ref = refs[4:7]
    rest = refs[7:]
    if with_router:
        wr_ref, rest = rest[0], rest[1:]
    xo_ref, h_ref = rest[:2]
    xo_ref[...] = x_new
    h = _rms(x_new, gn_ref[...]) * (1.0 + sc_ref[...]) + sh_ref[...]
    h_ref[...] = h.astype(h_ref.dtype)
    if with_router:
        _route(h, wr_ref, rest[2])


def _row_spec(d, off=0):
    return pl.BlockSpec((ROW_BLOCK, d), lambda i: (i + off, 0))


def _vec_spec(d):
    return pl.BlockSpec((1, d), lambda i: (0, 0))


def _mod_spec(d, n_ctx_blocks):
    return pl.BlockSpec((None, 1, d), lambda i: (jnp.where(i < n_ctx_blocks, 0, 1), 0, 0))


def _prenorm(x, g, shift, scale):
    m, d = x.shape
    nc = CTX_LEN // ROW_BLOCK
    return pl.pallas_call(
        _pre_body,
        grid=(m // ROW_BLOCK,),
        in_specs=[_row_spec(d), _vec_spec(d), _mod_spec(d, nc), _mod_spec(d, nc)],
        out_specs=_row_spec(d),
        out_shape=jax.ShapeDtypeStruct((m, d), BF16),
        compiler_params=_params("parallel"),
        name="prenorm",
    )(x, g.reshape(1, d), shift, scale)


def _postnorm(y, x, g_post, gate, nxt=None, w_router=None, latent_only=False):
    m, d = x.shape
    nc = CTX_LEN // ROW_BLOCK
    off = nc if latent_only else 0
    n_ctx = 0 if latent_only else nc
    m_out = m - off * ROW_BLOCK
    ins = [y, x, g_post.reshape(1, d), gate]
    in_specs = [_row_spec(d, off), _row_spec(d, off), _vec_spec(d), _mod_spec(d, n_ctx)]
    out_shape = [jax.ShapeDtypeStruct((m_out, d), F32)]
    out_specs = [_row_spec(d)]
    if nxt is not None:
        ins += [nxt[0].reshape(1, d), nxt[1], nxt[2]]
        in_specs += [_vec_spec(d), _mod_spec(d, n_ctx), _mod_spec(d, n_ctx)]
        out_shape.append(jax.ShapeDtypeStruct((m_out, d), BF16))
        out_specs.append(_row_spec(d))
        if w_router is not None:
            wr = jnp.zeros((d, LANES), F32).at[:, :N_EXPERTS].set(w_router)
            ins.append(wr)
            in_specs.append(pl.BlockSpec((d, LANES), lambda i: (0, 0)))
            out_shape.append(jax.ShapeDtypeStruct((m_out, LANES), F32))
            out_specs.append(pl.BlockSpec((ROW_BLOCK, LANES), lambda i: (i, 0)))
    out = pl.pallas_call(
        functools.partial(_post_body, with_next=nxt is not None, with_router=w_router is not None),
        grid=(m_out // ROW_BLOCK,),
        in_specs=in_specs,
        out_specs=out_specs,
        out_shape=out_shape,
        compiler_params=_params("parallel"),
        name="postnorm",
    )(*ins)
    return out


def _mm_body(a_ref, w_ref, o_ref, *, scaled_tiles, scale):
    acc = jnp.dot(a_ref[...], w_ref[...], preferred_element_type=F32)
    if scaled_tiles:
        acc = acc * jnp.where(pl.program_id(1) < scaled_tiles, scale, 1.0)
    o_ref[...] = acc.astype(o_ref.dtype)


def _matmul(a, w, layer, *, tm, tn, out_dtype, scaled_cols=0, scale=1.0):
    m, kk = a.shape
    n = w.shape[2]
    assert scaled_cols % tn == 0
    return pl.pallas_call(
        functools.partial(_mm_body, scaled_tiles=scaled_cols // tn, scale=scale),
        grid=(m // tm, n // tn),
        in_specs=[pl.BlockSpec((tm, kk), lambda i, j: (i, 0)),
                  pl.BlockSpec((None, kk, tn), lambda i, j: (layer, 0, j))],
        out_specs=pl.BlockSpec((tm, tn), lambda i, j: (i, j)),
        out_shape=jax.ShapeDtypeStruct((m, n), out_dtype),
        compiler_params=_params("parallel", "parallel"),
        name="matmul",
    )(a, w)


def _gated_body(a_ref, wg_ref, wu_ref, o_ref, *, act):
    a = a_ref[...]
    g = jnp.dot(a, wg_ref[...], preferred_element_type=F32)
    u = jnp.dot(a, wu_ref[...], preferred_element_type=F32)
    if act == "swiglu":
        r = g * jax.nn.sigmoid(g) * u
    else:
        r = g * jax.nn.sigmoid(u)
    o_ref[...] = r.astype(o_ref.dtype)


def _gated_matmul(a, w, layer, *, act, tm, tn, out_dtype):
    m, kk = a.shape
    f = w.shape[2] // 2
    nf = f // tn
    return pl.pallas_call(
        functools.partial(_gated_body, act=act),
        grid=(m // tm, nf),
        in_specs=[pl.BlockSpec((tm, kk), lambda i, j: (i, 0)),
                  pl.BlockSpec((None, kk, tn), lambda i, j: (layer, 0, j)),
                  pl.BlockSpec((None, kk, tn), lambda i, j: (layer, 0, nf + j))],
        out_specs=pl.BlockSpec((tm, tn), lambda i, j: (i, j)),
        out_shape=jax.ShapeDtypeStruct((m, f), out_dtype),
        compiler_params=_params("parallel", "parallel"),
        name="gated_matmul",
    )(a, w, w)


def _moe_routing(gates, tm):
    m = gates.shape[0]
    g8 = gates[:, :N_EXPERTS]
    sel = g8 > 0.0
    pos = jnp.cumsum(sel.astype(jnp.int32), axis=0) - 1
    tiles_e = (pos[-1] + tm) // tm
    tile_end = jnp.cumsum(tiles_e)
    slot = ((tile_end - tiles_e) * tm)[None, :] + pos
    n_tiles = (2 * m + tm - 1) // tm + N_EXPERTS
    n_slots = n_tiles * tm
    tile_expert = jnp.minimum(jnp.searchsorted(tile_end, jnp.arange(n_tiles, dtype=jnp.int32), side="right"),
                              N_EXPERTS - 1).astype(jnp.int32)
    n_used = tile_end[-1:].astype(jnp.int32)
    e1 = jnp.argmax(g8, axis=1)
    w1 = jnp.max(g8, axis=1)
    rest = jnp.where(jnp.arange(N_EXPERTS)[None, :] == e1[:, None], -1.0, g8)
    e2 = jnp.argmax(rest, axis=1)
    w2 = jnp.maximum(jnp.max(rest, axis=1), 0.0)
    slot1 = jnp.take_along_axis(slot, e1[:, None], axis=1)[:, 0].astype(jnp.int32)
    slot2 = jnp.take_along_axis(slot, e2[:, None], axis=1)[:, 0].astype(jnp.int32)
    two = w2 > 0.0
    weights = jnp.zeros((m, LANES), F32).at[:, 0].set(w1).at[:, 1].set(w2)
    return (tile_expert, n_used, slot1, jnp.where(two, slot2, n_slots), jnp.where(two, slot2, slot1), weights,
            n_slots)


def _scatter_body(s1_ref, s2_ref, h_ref, xs_in, xs_out, buf, sem):
    del xs_in
    base = pl.program_id(0) * ROW_BLOCK
    bits = lax.bitcast_convert_type(h_ref[...].astype(F32), jnp.uint32)
    half = bits.shape[1] // 2
    buf[...] = bits[:, :half] | (bits[:, half:] >> 16)

    def copies(r, row1, row2):
        return (pltpu.make_async_copy(buf.at[pl.ds(r, 1), :], xs_out.at[pl.ds(row1, 1), :], sem),
                pltpu.make_async_copy(buf.at[pl.ds(r, 1), :], xs_out.at[pl.ds(row2, 1), :], sem))

    def start(r, c):
        for cp in copies(r, s1_ref[base + r], s2_ref[base + r]):
            cp.start()
        return c

    def wait(r, c):
        for cp in copies(r, 0, 0):
            cp.wait()
        return c

    lax.fori_loop(0, ROW_BLOCK, start, 0, unroll=DMA_LOOP_UNROLL)
    lax.fori_loop(0, ROW_BLOCK, wait, 0, unroll=DMA_LOOP_UNROLL)


def _moe_scatter(h, slot1, slot2, n_slots):
    m, d = h.shape
    spare = 8
    xs0 = jnp.zeros((n_slots + spare, d // 2), jnp.uint32)
    return pl.pallas_call(
        _scatter_body,
        grid_spec=pltpu.PrefetchScalarGridSpec(
            num_scalar_prefetch=2,
            grid=(m // ROW_BLOCK,),
            in_specs=[pl.BlockSpec((ROW_BLOCK, d), lambda i, s1, s2: (i, 0)),
                      pl.BlockSpec(memory_space=pl.ANY)],
            out_specs=pl.BlockSpec(memory_space=pl.ANY),
            scratch_shapes=[pltpu.VMEM((ROW_BLOCK, d // 2), jnp.uint32), pltpu.SemaphoreType.DMA]),
        out_shape=jax.ShapeDtypeStruct(xs0.shape, jnp.uint32),
        input_output_aliases={3: 0},
        compiler_params=_params("arbitrary"),
        name="moe_scatter",
    )(slot1, slot2, h, xs0)


def _row_copy(src_hbm, dst, src_row, r, sem):
    return pltpu.make_async_copy(src_hbm.at[pl.ds(src_row, 1), :], dst.at[pl.ds(r, 1), :], sem)


def _row_gather(src_hbm, gathers, base, n, sem):
    def start(r, c):
        for dst, idx_ref in gathers:
            _row_copy(src_hbm, dst, idx_ref[base + r], r, sem).start()
        return c

    def wait(r, c):
        for dst, _ in gathers:
            _row_copy(src_hbm, dst, 0, r, sem).wait()
        return c

    lax.fori_loop(0, n, start, 0, unroll=DMA_LOOP_UNROLL)
    lax.fori_loop(0, n, wait, 0, unroll=DMA_LOOP_UNROLL)


def _grouped_gated_body(te_ref, used_ref, a_ref, wg_ref, wu_ref, o_ref):
    @pl.when(pl.program_id(0) < used_ref[0])
    def _():
        words = a_ref[...]
        half = words.shape[1]
        hi = lax.bitcast_convert_type(words & jnp.uint32(0xFFFF0000), F32).astype(BF16)
        lo = lax.bitcast_convert_type(words << 16, F32).astype(BF16)
        g = (jnp.dot(hi, wg_ref[:half, :], preferred_element_type=F32)
             + jnp.dot(lo, wg_ref[half:, :], preferred_element_type=F32))
        u = (jnp.dot(hi, wu_ref[:half, :], preferred_element_type=F32)
             + jnp.dot(lo, wu_ref[half:, :], preferred_element_type=F32))
        o_ref[...] = (g * jax.nn.sigmoid(g) * u).astype(o_ref.dtype)

    @pl.when(pl.program_id(0) >= used_ref[0])
    def _():
        o_ref[...] = jnp.zeros_like(o_ref)


def _grouped_swiglu(xs, w, first, tile_expert, n_used, *, tm, tn):
    kk = w.shape[1]
    n_tiles = tile_expert.shape[0]
    s = n_tiles * tm
    f = w.shape[2] // 2
    nf = f // tn
    return pl.pallas_call(
        _grouped_gated_body,
        grid_spec=pltpu.PrefetchScalarGridSpec(
            num_scalar_prefetch=2,
            grid=(n_tiles, nf),
            in_specs=[pl.BlockSpec((tm, kk // 2), lambda i, j, te, used: (i, 0)),
                      pl.BlockSpec((None, kk, tn), lambda i, j, te, used: (first + te[i], 0, j)),
                      pl.BlockSpec((None, kk, tn), lambda i, j, te, used: (first + te[i], 0, nf + j))],
            out_specs=pl.BlockSpec((tm, tn), lambda i, j, te, used: (i, j))),
        out_shape=jax.ShapeDtypeStruct((s, f), BF16),
        compiler_params=_params("parallel", "parallel"),
        name="moe_grouped_swiglu",
    )(tile_expert, n_used, xs, w, w)


def _grouped_mm_body(te_ref, used_ref, a_ref, w_ref, o_ref):
    @pl.when(pl.program_id(0) < used_ref[0])
    def _():
        o_ref[...] = jnp.dot(a_ref[...], w_ref[...], preferred_element_type=F32).astype(o_ref.dtype)

    @pl.when(pl.program_id(0) >= used_ref[0])
    def _():
        o_ref[...] = jnp.zeros_like(o_ref)


def _grouped_matmul(a, w, first, tile_expert, n_used, *, tm, tn, out_dtype):
    s, kk = a.shape
    n = w.shape[2]
    return pl.pallas_call(
        _grouped_mm_body,
        grid_spec=pltpu.PrefetchScalarGridSpec(
            num_scalar_prefetch=2,
            grid=(s // tm, n // tn),
            in_specs=[pl.BlockSpec((tm, kk), lambda i, j, te, used: (i, 0)),
                      pl.BlockSpec((None, kk, tn), lambda i, j, te, used: (first + te[i], 0, j))],
            out_specs=pl.BlockSpec((tm, tn), lambda i, j, te, used: (i, j))),
        out_shape=jax.ShapeDtypeStruct((s, n), out_dtype),
        compiler_params=_params("parallel", "parallel"),
        name="moe_grouped_matmul",
    )(tile_expert, n_used, a, w)


def _post_moe_body(s1_ref, s2_ref, ys_hbm, x_ref, wts_ref, gp_ref, gate_ref, *rest, with_next, off):
    if with_next:
        gn_ref, sh_ref, sc_ref, xo_ref, h_ref, buf, sem = rest
    else:
        xo_ref, buf, sem = rest
    base = (pl.program_id(0) + off) * ROW_BLOCK
    _row_gather(ys_hbm, [(buf.at[0], s1_ref), (buf.at[1], s2_ref)], base, ROW_BLOCK, sem)
    wts = wts_ref[...]
    lane = lax.broadcasted_iota(jnp.int32, wts.shape, 1)
    w1 = jnp.sum(jnp.where(lane == 0, wts, 0.0), axis=1, keepdims=True)
    w2 = jnp.sum(jnp.where(lane == 1, wts, 0.0), axis=1, keepdims=True)
    y = w1 * buf[0] + w2 * buf[1]
    x_new = x_ref[...] + gate_ref[...] * _rms(y, gp_ref[...])
    xo_ref[...] = x_new
    if with_next:
        h = _rms(x_new, gn_ref[...]) * (1.0 + sc_ref[...]) + sh_ref[...]
        h_ref[...] = h.astype(h_ref.dtype)


def _postnorm_moe(ys, slot1, slot2, weights, x, g_post, gate, nxt=None, latent_only=False):
    m, d = x.shape
    nc = CTX_LEN // ROW_BLOCK
    off = nc if latent_only else 0
    n_ctx = 0 if latent_only else nc
    m_out = m - off * ROW_BLOCK
    row = lambda w: pl.BlockSpec((ROW_BLOCK, w), lambda i, s1, s2: (i + off, 0))
    vec = pl.BlockSpec((1, d), lambda i, s1, s2: (0, 0))
    modv = pl.BlockSpec((None, 1, d), lambda i, s1, s2: (jnp.where(i < n_ctx, 0, 1), 0, 0))
    out_row = lambda: pl.BlockSpec((ROW_BLOCK, d), lambda i, s1, s2: (i, 0))
    ins = [ys, x, weights, g_post.reshape(1, d), gate]
    in_specs = [pl.BlockSpec(memory_space=pl.ANY), row(d), row(LANES), vec, modv]
    out_shape = [jax.ShapeDtypeStruct((m_out, d), F32)]
    out_specs = [out_row()]
    if nxt is not None:
        ins += [nxt[0].reshape(1, d), nxt[1], nxt[2]]
        in_specs += [vec, modv, modv]
        out_shape.append(jax.ShapeDtypeStruct((m_out, d), BF16))
        out_specs.append(out_row())
    return pl.pallas_call(
        functools.partial(_post_moe_body, with_next=nxt is not None, off=off),
        grid_spec=pltpu.PrefetchScalarGridSpec(
            num_scalar_prefetch=2,
            grid=(m_out // ROW_BLOCK,),
            in_specs=in_specs,
            out_specs=out_specs,
            scratch_shapes=[pltpu.VMEM((2, ROW_BLOCK, d), F32), pltpu.SemaphoreType.DMA]),
        out_shape=out_shape,
        compiler_params=_params("arbitrary"),
        name="postnorm_moe",
    )(slot1, slot2, *ins)


def _s5_lag_body(b_ref, c_ref, o_ref):
    for g in range(b_ref.shape[0]):
        o_ref[g] = jnp.dot(b_ref[g], c_ref[g], preferred_element_type=F32,
                           precision=lax.Precision.HIGHEST)


def _s5_lag_kernels(b_lag, c_cat):
    g, r, n2 = b_lag.shape
    gb = S5_SLAB_GROUPS
    return pl.pallas_call(
        _s5_lag_body,
        grid=(g // gb,),
        in_specs=[pl.BlockSpec((gb, r, n2), lambda i: (i, 0, 0)),
                  pl.BlockSpec((gb, n2, S5_GROUP), lambda i: (i, 0, 0))],
        out_specs=pl.BlockSpec((gb, r, S5_GROUP), lambda i: (i, 0, 0)),
        out_shape=jax.ShapeDtypeStruct((g, r, S5_GROUP), F32),
        compiler_params=_params("parallel"),
        name="s5_lag_kernels",
    )(b_lag, c_cat)


def _chunk_tile(nc):
    return S5_CHUNK_TILE if nc % S5_CHUNK_TILE == 0 else nc


def _slab_rows(g, k):
    return pl.ds(k * LANES + g * S5_GROUP, S5_GROUP)


def _s5_in_body(u_ref, bc_ref, vre_ref, vim_ref, w_ref, *, tc):
    first = jnp.logical_and(pl.program_id(0) == 0, pl.program_id(1) == 0)

    @pl.when(first)
    def _():
        w_ref[...] = jnp.zeros_like(w_ref)

    @pl.when(pl.program_id(1) == 0)
    def _():
        for g in range(S5_SLAB_GROUPS):
            for k in range(S5_CHUNK):
                w_ref[g, _slab_rows(g, k), :] = bc_ref[g, k * S5_GROUP:(k + 1) * S5_GROUP, :]

    u = u_ref[...]
    for g in range(S5_SLAB_GROUPS):
        v = jnp.dot(u, w_ref[g], preferred_element_type=F32)
        rows = pl.ds(g, tc, stride=S5_SLAB_GROUPS)
        vre_ref[rows, :] = v[:, :LANES]
        vim_ref[rows, :] = v[:, LANES:]


def _s5_state_inputs(u, bc):
    ns, nc, sw = u.shape
    tc = _chunk_tile(nc)
    gb = S5_SLAB_GROUPS
    n4 = 4 * S5_STATE
    out = pl.BlockSpec((None, tc * gb, LANES), lambda s, m: (s, m, 0))
    return pl.pallas_call(
        functools.partial(_s5_in_body, tc=tc),
        grid=(ns, nc // tc),
        in_specs=[pl.BlockSpec((None, tc, sw), lambda s, m: (s, m, 0)),
                  pl.BlockSpec((gb, S5_CK, n4), lambda s, m: (s, 0, 0))],
        out_specs=[out, out],
        out_shape=[jax.ShapeDtypeStruct((ns, nc * gb, LANES), F32)] * 2,
        scratch_shapes=[pltpu.VMEM((gb, sw, n4), BF16)],
        compiler_params=_params("arbitrary", "arbitrary"),
        name="s5_state_inputs",
    )(u, bc)


def _s5_scan_body(vre_ref, vim_ref, a_ref, sre_ref, sim_ref, *, nc, nctx):
    gb = S5_SLAB_GROUPS
    a_re, a_im = a_ref[:, :LANES], a_ref[:, LANES:]
    fwd = lax.broadcasted_iota(jnp.int32, (gb, LANES), 1) < S5_STATE
    bwd = jnp.logical_not(fwd)

    def step(t, carry):
        s_re, s_im = carry
        t_b = jnp.where(t < nctx, nctx - 1 - t, nc - 1 - (t - nctx))
        rf = pl.ds(pl.multiple_of(t * gb, gb), gb)
        rb = pl.ds(pl.multiple_of(t_b * gb, gb), gb)
        pltpu.store(sre_ref.at[rf, :], s_re, mask=fwd)
        pltpu.store(sre_ref.at[rb, :], s_re, mask=bwd)
        pltpu.store(sim_ref.at[rf, :], s_im, mask=fwd)
        pltpu.store(sim_ref.at[rb, :], s_im, mask=bwd)
        v_re = jnp.where(fwd, vre_ref[rf, :], vre_ref[rb, :])
        v_im = jnp.where(fwd, vim_ref[rf, :], vim_ref[rb, :])
        return a_re * s_re - a_im * s_im + v_re, a_re * s_im + a_im * s_re + v_im

    zero = jnp.zeros((gb, LANES), F32)
    lax.fori_loop(0, nc, step, (zero, zero))


def _s5_scan(v_re, v_im, a_chunk):
    ns, rows, _ = v_re.shape
    gb = S5_SLAB_GROUPS
    nc = rows // gb
    blk = pl.BlockSpec((None, rows, LANES), lambda s: (s, 0, 0))
    return pl.pallas_call(
        functools.partial(_s5_scan_body, nc=nc, nctx=CTX_LEN // S5_CHUNK),
        grid=(ns,),
        in_specs=[blk, blk, pl.BlockSpec((None, gb, 2 * LANES), lambda s: (s, 0, 0))],
        out_specs=[blk, blk],
        out_shape=[jax.ShapeDtypeStruct((ns, rows, LANES), F32)] * 2,
        compiler_params=_params("parallel"),
        name="s5_scan",
    )(v_re, v_im, a_chunk)


def _s5_out_body(u_ref, sre_ref, sim_ref, t_ref, c_ref, y_ref, wt_ref, wc_ref, scat_ref, *, tc):
    gb = S5_SLAB_GROUPS
    n4 = 4 * S5_STATE

    @pl.when(pl.program_id(1) == 0)
    def _():
        row = lax.broadcasted_iota(jnp.int32, (S5_CK, S5_SLAB_W), 0)
        col = lax.broadcasted_iota(jnp.int32, (S5_CK, S5_SLAB_W), 1)
        spread = (row // S5_GROUP) * LANES + row % S5_GROUP
        for g in range(gb):
            place = (col == spread + g * S5_GROUP).astype(BF16)
            tg = jnp.dot(t_ref[g], place, preferred_element_type=F32).astype(BF16)
            for k in range(S5_CHUNK):
                wt_ref[_slab_rows(g, k), :] = tg[k * S5_GROUP:(k + 1) * S5_GROUP, :]
            wc_ref[g * n4:(g + 1) * n4, :] = jnp.dot(c_ref[g], place, preferred_element_type=F32).astype(BF16)

    for g in range(gb):
        rows = pl.ds(g, tc, stride=gb)
        scat_ref[:, g * n4:g * n4 + LANES] = sre_ref[rows, :].astype(BF16)
        scat_ref[:, g * n4 + LANES:(g + 1) * n4] = sim_ref[rows, :].astype(BF16)
    y = (jnp.dot(u_ref[...], wt_ref[...], preferred_element_type=F32)
         + jnp.dot(scat_ref[...], wc_ref[...], preferred_element_type=F32))
    y_ref[...] = jax.nn.gelu(y).astype(y_ref.dtype)


def _s5_outputs(u, s_re, s_im, toep, cc):
    ns, nc, sw = u.shape
    tc = _chunk_tile(nc)
    gb = S5_SLAB_GROUPS
    n4 = 4 * S5_STATE
    state = pl.BlockSpec((None, tc * gb, LANES), lambda i, m: (i, m, 0))
    return pl.pallas_call(
        functools.partial(_s5_out_body, tc=tc),
        grid=(ns, nc // tc),
        in_specs=[pl.BlockSpec((None, tc, sw), lambda i, m: (i, m, 0)), state, state,
                  pl.BlockSpec((gb, S5_CK, S5_CK), lambda i, m: (i, 0, 0)),
                  pl.BlockSpec((gb, n4, S5_CK), lambda i, m: (i, 0, 0))],
        out_specs=pl.BlockSpec((None, tc, sw), lambda i, m: (i, m, 0)),
        out_shape=jax.ShapeDtypeStruct((ns, nc, sw), BF16),
        scratch_shapes=[pltpu.VMEM((sw, sw), BF16), pltpu.VMEM((gb * n4, sw), BF16),
                        pltpu.VMEM((tc, gb * n4), BF16)],
        compiler_params=_params("parallel", "arbitrary"),
        name="s5_outputs",
    )(u, s_re, s_im, toep, cc)


def _s5_operators(a_re, a_im, log_dt, b_re, b_im, c_re, c_im, d_skip):
    L, G, N, C = S5_CHUNK, S5_GROUPS, S5_STATE, S5_GROUP
    a_re, a_im = a_re.astype(F32), a_im.astype(F32)
    b_re, b_im = b_re.astype(F32), b_im.astype(F32)
    c_re, c_im = c_re.astype(F32), c_im.astype(F32)
    dt = jnp.exp(log_dt.astype(F32))[..., None]
    steps = jnp.arange(L + 1, dtype=F32)[:, None, None, None]
    mag = jnp.exp((a_re * dt)[None] * steps)
    ang = (a_im * dt)[None] * steps
    pw_re, pw_im = mag * jnp.cos(ang), mag * jnp.sin(ang)
    p, q = pw_re[1] - 1.0, pw_im[1]
    den = a_re * a_re + a_im * a_im
    f_re, f_im = (p * a_re + q * a_im) / den, (q * a_re - p * a_im) / den
    bb_re = f_re[..., None] * b_re - f_im[..., None] * b_im
    bb_im = f_re[..., None] * b_im + f_im[..., None] * b_re

    def cmul(xr, xi, yr, yi):
        return xr * yr - xi * yi, xr * yi + xi * yr

    bl_re, bl_im = cmul(pw_re[:L, :, :, :, None], pw_im[:L, :, :, :, None], bb_re[None], bb_im[None])
    lag_rows = lambda z: jnp.transpose(z, (1, 2, 0, 4, 3)).reshape(2 * G, L * C, N)
    b_lag = jnp.concatenate([lag_rows(bl_re), -lag_rows(bl_im)], axis=-1)
    c_t = lambda z: jnp.swapaxes(z, -1, -2).reshape(2 * G, N, C)
    c_cat = jnp.concatenate([c_t(c_re), c_t(c_im)], axis=1)
    k_lag = _s5_lag_kernels(b_lag, c_cat).reshape(2, G, L, C, C)

    k_idx = jnp.arange(L)
    lag = k_idx[None, :] - k_idx[:, None]
    kf = jnp.where((lag >= 0)[None, :, :, None, None], k_lag[0][:, jnp.clip(lag, 0, L - 1)], 0.0)
    kr = jnp.where((lag <= 0)[None, :, :, None, None], k_lag[1][:, jnp.clip(-lag, 0, L - 1)], 0.0)
    eye_k = jnp.eye(L, dtype=F32)[None, :, :, None, None]
    d_diag = d_skip.astype(F32).reshape(G, 1, 1, C, 1) * jnp.eye(C, dtype=F32)[None, None, None]
    toep = kf + kr + eye_k * d_diag
    toep = jnp.transpose(toep, (0, 1, 3, 2, 4)).reshape(G, L * C, L * C)

    bf_re, bf_im = cmul(pw_re[L - 1 - k_idx, 0][..., None], pw_im[L - 1 - k_idx, 0][..., None],
                        bb_re[0][None], bb_im[0][None])
    bk_re, bk_im = cmul(pw_re[k_idx, 1][..., None], pw_im[k_idx, 1][..., None],
                        bb_re[1][None], bb_im[1][None])
    to_rows = lambda z: jnp.transpose(z, (1, 0, 3, 2)).reshape(G, L * C, N)
    bc = jnp.concatenate([to_rows(bf_re), to_rows(bk_re), to_rows(bf_im), to_rows(bk_im)], axis=-1)

    pw_g = lambda z, idx, d: jnp.transpose(z[idx, d], (1, 0, 2))[:, :, None, :]
    cf_re, cf_im = cmul(c_re[0][:, None], c_im[0][:, None], pw_g(pw_re, k_idx + 1, 0), pw_g(pw_im, k_idx + 1, 0))
    cb_re, cb_im = cmul(c_re[1][:, None], c_im[1][:, None], pw_g(pw_re, L - k_idx, 1), pw_g(pw_im, L - k_idx, 1))
    to_cols = lambda z: jnp.transpose(z, (0, 3, 1, 2)).reshape(G, N, L * C)
    cc = jnp.concatenate([to_cols(cf_re), to_cols(cb_re), -to_cols(cf_im), -to_cols(cb_im)], axis=1)

    a_chunk = jnp.concatenate([pw_re[L, 0], pw_re[L, 1], pw_im[L, 0], pw_im[L, 1]], axis=-1)
    a_chunk = a_chunk.reshape(S5_SLABS, S5_SLAB_GROUPS, 4 * N)
    return toep.astype(BF16), bc.astype(BF16), cc.astype(BF16), a_chunk


def _s5_mixer(h, ops):
    toep, bc, cc, a_chunk = ops
    m, d = h.shape
    nc = m // S5_CHUNK
    u = jnp.transpose(h.reshape(nc, S5_CHUNK, S5_SLABS, LANES), (2, 0, 1, 3)).reshape(S5_SLABS, nc, S5_SLAB_W)
    v_re, v_im = _s5_state_inputs(u, bc)
    s_re, s_im = _s5_scan(v_re, v_im, a_chunk)
    y = _s5_outputs(u, s_re, s_im, toep, cc)
    return jnp.transpose(y.reshape(S5_SLABS, nc, S5_CHUNK, LANES), (1, 2, 0, 3)).reshape(m, d)


def _na_bias_tables(rpb):
    var = jnp.arange(NA_KH)
    row_rel = jnp.arange(NA_KH)[None, :] - var[:, None] + (NA_KH - 1)
    q = jnp.arange(GRID_W)
    col_start = jnp.clip(q - NA_KW // 2, 0, GRID_W - NA_KW)
    kc = jnp.arange(GRID_W)
    inside = (kc[None, :] >= col_start[:, None]) & (kc[None, :] < col_start[:, None] + NA_KW)
    col_rel = jnp.clip(kc[None, :] - q[:, None] + (NA_KW - 1), 0, 2 * NA_KW - 2)
    t = (rpb.astype(F32) * LOG2_E)[:, row_rel][:, :, :, col_rel]
    t = jnp.where(inside[None, None, None], t, NA_NEG)
    t = jnp.transpose(t, (1, 0, 3, 2, 4)).reshape(NA_KH, rpb.shape[0], GRID_W, NA_KH * GRID_W)

    def step_table(variants, offsets):
        rows = [jnp.pad(t[v], ((0, 0), (0, 0), (o * GRID_W, (NA_BAND - NA_KH - o) * GRID_W)),
                        constant_values=NA_NEG) for v, o in zip(variants, offsets)]
        return jnp.concatenate(rows, axis=1)

    r = NA_ROWS_PER_STEP
    half = NA_KH // 2
    first = step_table(range(r), [0] * r)
    inner = step_table([half] * r, range(r))
    last = step_table(range(half, half + r), [NA_BAND - NA_KH] * r)
    return jnp.stack([first, inner, last], axis=1)


def _na_body(q_ref, k_ref, v_ref, bias_ref, o_ref, *, rows):
    i = pl.program_id(1)
    dn = (((1,), (1,)), ((), ()))
    nctx = CTX_LEN // (NA_ROWS_PER_STEP * GRID_W)
    band = NA_BAND * GRID_W
    heads = [pl.ds(hd * NA_HEAD_DIM, NA_HEAD_DIM) for hd in range(NA_HEADS_PER_STEP)]

    def ctx_scores(hd):
        return lax.dot_general(q_ref[:, hd], k_ref[0:CTX_LEN, hd], dn, preferred_element_type=F32)

    @pl.when(i < nctx)
    def _():
        for hd in heads:
            s_c = ctx_scores(hd)
            p = jnp.exp2(s_c - jnp.max(s_c, axis=-1, keepdims=True))
            o = jnp.dot(p.astype(BF16), v_ref[0:CTX_LEN, hd], preferred_element_type=F32)
            o_ref[:, hd] = (o / jnp.sum(p, axis=-1, keepdims=True)).astype(o_ref.dtype)

    @pl.when(i >= nctx)
    def _():
        r = (i - nctx) * NA_ROWS_PER_STEP
        rb = jnp.clip(r - NA_KH // 2, 0, rows - NA_BAND)
        kind = jnp.where(r == 0, 0, jnp.where(r == rows - NA_ROWS_PER_STEP, 2, 1))
        start = pl.multiple_of(CTX_LEN + rb * GRID_W, GRID_W)
        for n, hd in enumerate(heads):
            s_c = ctx_scores(hd)
            s_w = lax.dot_general(q_ref[:, hd], k_ref[pl.ds(start, band), hd], dn,
                                  preferred_element_type=F32) + bias_ref[n, kind]
            mx = jnp.maximum(jnp.max(s_w, axis=-1, keepdims=True), jnp.max(s_c, axis=-1, keepdims=True))
            p_w = jnp.exp2(s_w - mx)
            p_c = jnp.exp2(s_c - mx)
            den = jnp.sum(p_w, axis=-1, keepdims=True) + jnp.sum(p_c, axis=-1, keepdims=True)
            o = (jnp.dot(p_w.astype(BF16), v_ref[pl.ds(start, band), hd], preferred_element_type=F32)
                 + jnp.dot(p_c.astype(BF16), v_ref[0:CTX_LEN, hd], preferred_element_type=F32))
            o_ref[:, hd] = (o / den).astype(o_ref.dtype)


def _na_attention(qkv, bias):
    m = qkv.shape[0]
    rows = (m - CTX_LEN) // GRID_W
    assert rows % NA_ROWS_PER_STEP == 0 and rows >= NA_BAND + NA_ROWS_PER_STEP
    tq = NA_ROWS_PER_STEP * GRID_W
    hs = NA_HEADS_PER_STEP
    w = hs * NA_HEAD_DIM
    nh = NA_HEADS // hs
    return pl.pallas_call(
        functools.partial(_na_body, rows=rows),
        grid=(nh, m // tq),
        in_specs=[pl.BlockSpec((tq, w), lambda h, i: (i, h)),
                  pl.BlockSpec((m, w), lambda h, i: (0, nh + h), pipeline_mode=pl.Buffered(1)),
                  pl.BlockSpec((m, w), lambda h, i: (0, 2 * nh + h), pipeline_mode=pl.Buffered(1)),
                  pl.BlockSpec((hs, 3, tq, NA_BAND * GRID_W), lambda h, i: (h, 0, 0, 0),
                               pipeline_mode=pl.Buffered(1))],
        out_specs=pl.BlockSpec((tq, w), lambda h, i: (i, h)),
        out_shape=jax.ShapeDtypeStruct((m, D_MODEL), BF16),
        compiler_params=_params("parallel", "arbitrary"),
        name="na_attention",
    )(qkv, qkv, qkv, bias)


def _tiles(m):
    for blocks in range(5, 0, -1):
        if m % (blocks * ROW_BLOCK) == 0:
            return blocks * ROW_BLOCK
    raise ValueError(m)


def kernel(x, c, ctx, c_ctx, ada_w, ada_b, norm_g, s5_a_re, s5_a_im, s5_log_dt, s5_b_re, s5_b_im,
           s5_c_re, s5_c_im, s5_d, s5_w_glu, na_w_qkv, na_w_o, na_rpb, ffn_w_gu, ffn_w_down,
           moe_w_router, moe_w_gu, moe_w_down):
    assert x.shape[0] == 1 and x.shape[2] == D_MODEL and ctx.shape[1] == CTX_LEN
    d = D_MODEL
    xc = jnp.concatenate([ctx[0], x[0]], axis=0).astype(F32)
    m = xc.shape[0]
    tm = _tiles(m)

    cond_t = jnp.stack([c_ctx, c[0]], axis=1).astype(F32)
    mods = _adaln(cond_t, ada_w.astype(F32), ada_b.astype(F32))
    mods = mods.reshape(DEPTH, 2, 6, 1, d)
    mod = lambda i, k: mods[i, :, k]

    w_glu, w_ffn_gu, w_ffn_down = s5_w_glu.astype(BF16), ffn_w_gu.astype(BF16), ffn_w_down.astype(BF16)
    w_qkv, w_o = na_w_qkv.astype(BF16), na_w_o.astype(BF16)
    n_odd = moe_w_gu.shape[0]
    w_moe_gu = moe_w_gu.astype(BF16).reshape(n_odd * N_EXPERTS, d, 2 * EXPERT_DIM)
    w_moe_down = moe_w_down.astype(BF16).reshape(n_odd * N_EXPERTS, EXPERT_DIM, d)

    h = _prenorm(xc, norm_g[0, 0], mod(0, 0), mod(0, 1))
    for i in range(DEPTH):
        j = i // 2
        last = i == DEPTH - 1
        nxt_f = (norm_g[i, 2], mod(i, 3), mod(i, 4))
        if i % 2 == 0:
            ops = _s5_operators(s5_a_re[j], s5_a_im[j], s5_log_dt[j], s5_b_re[j], s5_b_im[j],
                                s5_c_re[j], s5_c_im[j], s5_d[j])
            g = _s5_mixer(h, ops)
            y = _gated_matmul(g, w_glu, j, act="glu", tm=tm, tn=512, out_dtype=BF16)
            xc, h = _postnorm(y, xc, norm_g[i, 1], mod(i, 2), nxt_f)
            a = _gated_matmul(h, w_ffn_gu, j, act="swiglu", tm=tm, tn=512, out_dtype=BF16)
            y = _matmul(a, w_ffn_down, j, tm=tm, tn=1024, out_dtype=BF16)
        else:
            qkv = _matmul(h, w_qkv, j, tm=tm, tn=1024, out_dtype=BF16, scaled_cols=d, scale=NA_Q_SCALE)
            o = _na_attention(qkv, _na_bias_tables(na_rpb[j]))
            y = _matmul(o, w_o, j, tm=tm, tn=1024, out_dtype=BF16)
            xc, h, gates = _postnorm(y, xc, norm_g[i, 1], mod(i, 2), nxt_f, w_router=moe_w_router[j])
            tile_expert, n_used, slot1, slot2_scatter, slot2, weights, n_slots = _moe_routing(gates, tm)
            xs = _moe_scatter(h, slot1, slot2_scatter, n_slots)
            a = _grouped_swiglu(xs, w_moe_gu, j * N_EXPERTS, tile_expert, n_used, tm=tm, tn=512)
            ys = _grouped_matmul(a, w_moe_down, j * N_EXPERTS, tile_expert, n_used, tm=tm, tn=2048,
                                 out_dtype=F32)
            if last:
                out, = _postnorm_moe(ys, slot1, slot2, weights, xc, norm_g[i, 3], mod(i, 5), latent_only=True)
                return out[None].astype(x.dtype)
            nxt_m = (norm_g[i + 1, 0], mod(i + 1, 0), mod(i + 1, 1))
            xc, h = _postnorm_moe(ys, slot1, slot2, weights, xc, norm_g[i, 3], mod(i, 5), nxt_m)
            continue
        nxt_m = (norm_g[i + 1, 0], mod(i + 1, 0), mod(i + 1, 1))
        xc, h = _postnorm(y, xc, norm_g[i, 3], mod(i, 5), nxt_m)
```
